```python
import math
import jax, jax.numpy as jnp
from jax import lax
import numpy as np

D_MODEL = 1024
BATCH = 8
SEQ = 2048
DEPTH = 4

CHUNK = 64
N_MIXERS = 2
N_SSD_LAYERS = (DEPTH + 1) // 2
N_ATTN_LAYERS = DEPTH // 2
EPS = 1e-6

SSD_EXPAND = 2
SSD_D_INNER = SSD_EXPAND * D_MODEL
SSD_HEAD_DIM = 64
SSD_HEADS = SSD_D_INNER // SSD_HEAD_DIM
SSD_GROUPS = 4
SSD_HEADS_PER_GROUP = SSD_HEADS // SSD_GROUPS
SSD_STATE = 128
SSD_CONV = 4
SSD_CONV_CH = SSD_D_INNER + 2 * SSD_GROUPS * SSD_STATE
SSD_IN = SSD_D_INNER + SSD_CONV_CH + SSD_HEADS

ATTN_HEADS = 8
ATTN_HEAD_DIM = D_MODEL // ATTN_HEADS // 2
ATTN_V_DIM = 2 * ATTN_HEAD_DIM
ATTN_IN = 3 * D_MODEL
ROPE_THETA = 500000.0
ROPE_DIM = ATTN_HEAD_DIM // 4
Q_BLOCK = 128

D_FF = 2816
FFN_CONV = 3

kernel_name = "hybrid_ssd_diffattn_convffn_trunk"


def rmsnorm(x, g):
    xf = x.astype(jnp.float32)
    y = xf * lax.rsqrt(jnp.mean(xf * xf, axis=-1, keepdims=True) + EPS)
    return (y * g).astype(x.dtype)


def causal_dwconv(x, w, b):
    k_width = w.shape[0]
    s = x.shape[1]
    xp = jnp.pad(x, ((0, 0), (k_width - 1, 0), (0, 0)))
    y = b
    for k in range(k_width):
        y = y + w[k] * xp[:, k:k + s]
    return y


def rope_tables(positions):
    inv_freq = ROPE_THETA ** (-jnp.arange(0, ROPE_DIM, 2, dtype=jnp.float32) / ROPE_DIM)
    ang = positions.astype(jnp.float32)[..., None] * inv_freq
    return jnp.cos(ang), jnp.sin(ang)


def apply_partial_rope(t, cos, sin):
    half = ROPE_DIM // 2
    c = cos[:, :, None, None, :]
    s = sin[:, :, None, None, :]
    tf = t.astype(jnp.float32)
    x1, x2, rest = tf[..., :half], tf[..., half:ROPE_DIM], tf[..., ROPE_DIM:]
    out = jnp.concatenate([x1 * c - x2 * s, x2 * c + x1 * s, rest], axis=-1)
    return out.astype(t.dtype)


def ssd_mixer(xn, in_w, conv_w, conv_b, dt_bias, a_log, d_skip, norm_g, out_w):
    b, s, _ = xn.shape
    nc = s // CHUNK
    G, R, P, N = SSD_GROUPS, SSD_HEADS_PER_GROUP, SSD_HEAD_DIM, SSD_STATE
    proj = xn @ in_w
    z = proj[..., :SSD_D_INNER]
    xbc = proj[..., SSD_D_INNER:SSD_D_INNER + SSD_CONV_CH]
    dt = proj[..., SSD_D_INNER + SSD_CONV_CH:]
    xbc = jax.nn.silu(causal_dwconv(xbc, conv_w, conv_b))
    xs = xbc[..., :SSD_D_INNER].reshape(b, nc, CHUNK, G, R, P)
    Bm = xbc[..., SSD_D_INNER:SSD_D_INNER + G * N].reshape(b, nc, CHUNK, G, N)
    Cm = xbc[..., SSD_D_INNER + G * N:].reshape(b, nc, CHUNK, G, N)
    dt = jax.nn.softplus(dt + dt_bias).reshape(b, nc, CHUNK, G, R)
    a = -jnp.exp(a_log).reshape(G, R)
    dA = (dt * a).transpose(0, 3, 4, 1, 2)
    x_dt = xs * dt[..., None]
    a_cs = jnp.cumsum(dA, axis=-1)
    seg = a_cs[..., :, None] - a_cs[..., None, :]
    tril = jnp.tril(jnp.ones((CHUNK, CHUNK), dtype=bool))
    lmat = jnp.where(tril, jnp.exp(jnp.where(tril, seg, -jnp.inf)), 0.0)
    cb = jnp.einsum("bclgn,bcsgn->bgcls", Cm, Bm)
    y_diag = jnp.einsum("bgrcls,bcsgrp->bclgrp", cb[:, :, None] * lmat, x_dt)
    decay = jnp.exp(a_cs[..., -1:] - a_cs)
    states = jnp.einsum("bclgn,bgrcl,bclgrp->cbgrpn", Bm, decay, x_dt)
    chunk_decay = jnp.exp(a_cs[..., -1]).transpose(3, 0, 1, 2)

    def step(h, inp):
        st, dec = inp
        return dec[..., None, None] * h + st, h

    _, h_in = lax.scan(step, jnp.zeros_like(states[0]), (states, chunk_decay))
    y_off = jnp.einsum("bclgn,cbgrpn,bgrcl->bclgrp", Cm, h_in, jnp.exp(a_cs))
    y = y_diag + y_off + xs * d_skip.reshape(G, R)[:, :, None]
    y = y.reshape(b, s, SSD_D_INNER)
    yg = (y * jax.nn.silu(z)).reshape(b, s, G, SSD_D_INNER // G)
    yg = rmsnorm(yg, 1.0).reshape(b, s, SSD_D_INNER) * norm_g
    return yg.astype(xn.dtype) @ out_w


def diff_attention(xn, cos, sin, in_w, q_norm_g, k_norm_g, lq1, lk1, lq2, lk2,
                   subln_g, out_w, lambda_init):
    b, s, _ = xn.shape
    H, HD = ATTN_HEADS, ATTN_HEAD_DIM
    proj = xn @ in_w
    q = proj[..., :D_MODEL].reshape(b, s, H, 2, HD)
    k = proj[..., D_MODEL:2 * D_MODEL].reshape(b, s, H, 2, HD)
    v = proj[..., 2 * D_MODEL:].reshape(b, s, H, ATTN_V_DIM)
    q = apply_partial_rope(rmsnorm(q, q_norm_g), cos, sin)
    k = apply_partial_rope(rmsnorm(k, k_norm_g), cos, sin)
    lam = (jnp.exp(jnp.sum(lq1.astype(jnp.float32) * lk1.astype(jnp.float32)))
           - jnp.exp(jnp.sum(lq2.astype(jnp.float32) * lk2.astype(jnp.float32)))
           + lambda_init)
    scale = 1.0 / math.sqrt(HD)
    outs = []
    for blk in range(s // Q_BLOCK):
        q0, q1 = blk * Q_BLOCK, (blk + 1) * Q_BLOCK
        qb = q[:, q0:q1]
        kc = k[:, :q1]
        vc = v[:, :q1]
        sc = jnp.einsum("bqhcd,bkhcd->bhcqk", qb, kc).astype(jnp.float32) * scale
        q_chunk = (q0 + jnp.arange(Q_BLOCK)) // CHUNK
        k_chunk = jnp.arange(q1) // CHUNK
        allowed = k_chunk[None, :] <= q_chunk[:, None]
        sc = jnp.where(allowed, sc, -jnp.inf)
        p = jax.nn.softmax(sc, axis=-1)
        attn = p[:, :, 0] - lam * p[:, :, 1]
        outs.append(jnp.einsum("bhqk,bkhe->bqhe", attn.astype(v.dtype), vc))
    o = jnp.concatenate(outs, axis=1)
    o = rmsnorm(o, subln_g) * (1.0 - lambda_init)
    return o.reshape(b, s, H * ATTN_V_DIM).astype(xn.dtype) @ out_w


def conv_ffn(xn, up_w, conv_w, conv_b, down_w):
    h = causal_dwconv(xn @ up_w, conv_w, conv_b)
    g, u = h[..., :D_FF], h[..., D_FF:]
    return (jax.nn.silu(g) * u) @ down_w


def setup_inputs(seed: int = 0) -> dict:
    key = jax.random.key(seed)
    ks = jax.random.split(key, 32)
    f32 = jnp.float32

    def nrm(k, shape, std):
        return jax.random.normal(k, shape, f32) * std

    def gain(k, shape):
        return 1.0 + 0.05 * jax.random.normal(k, shape, f32)

    out_scale = 1.0 / math.sqrt(2 * DEPTH)
    x = jax.random.normal(ks[0], (BATCH, SEQ, D_MODEL), f32)
    start = jax.random.randint(ks[1], (BATCH,), 0, 64) * CHUNK
    positions = (start[:, None] + jnp.arange(SEQ)[None, :]).astype(jnp.int32)
    dt0 = jnp.exp(jax.random.uniform(ks[6], (N_SSD_LAYERS, SSD_HEADS), f32,
                                     math.log(0.001), math.log(0.1)))
    return {
        "x": x,
        "positions": positions,
        "norm_mix_g": gain(ks[2], (DEPTH, D_MODEL)),
        "norm_ffn_g": gain(ks[3], (DEPTH, D_MODEL)),
        "ssd_in_w": nrm(ks[4], (N_SSD_LAYERS, D_MODEL, SSD_IN), D_MODEL ** -0.5),
        "ssd_conv_w": nrm(ks[5], (N_SSD_LAYERS, SSD_CONV, SSD_CONV_CH), SSD_CONV ** -0.5),
        "ssd_conv_b": nrm(ks[7], (N_SSD_LAYERS, SSD_CONV_CH), 0.02),
        "ssd_dt_bias": dt0 + jnp.log(-jnp.expm1(-dt0)),
        "ssd_a_log": jnp.log(jax.random.uniform(ks[8], (N_SSD_LAYERS, SSD_HEADS), f32, 1.0, 16.0)),
        "ssd_d": gain(ks[9], (N_SSD_LAYERS, SSD_HEADS)),
        "ssd_norm_g": gain(ks[10], (N_SSD_LAYERS, SSD_D_INNER)),
        "ssd_out_w": nrm(ks[11], (N_SSD_LAYERS, SSD_D_INNER, D_MODEL), SSD_D_INNER ** -0.5 * out_scale),
        "attn_in_w": nrm(ks[12], (N_ATTN_LAYERS, D_MODEL, ATTN_IN), D_MODEL ** -0.5),
        "attn_q_norm_g": gain(ks[13], (N_ATTN_LAYERS, ATTN_HEAD_DIM)),
        "attn_k_norm_g": gain(ks[14], (N_ATTN_LAYERS, ATTN_HEAD_DIM)),
        "attn_lq1": nrm(ks[15], (N_ATTN_LAYERS, ATTN_HEAD_DIM), 0.1),
        "attn_lk1": nrm(ks[16], (N_ATTN_LAYERS, ATTN_HEAD_DIM), 0.1),
        "attn_lq2": nrm(ks[17], (N_ATTN_LAYERS, ATTN_HEAD_DIM), 0.1),
        "attn_lk2": nrm(ks[18], (N_ATTN_LAYERS, ATTN_HEAD_DIM), 0.1),
        "attn_subln_g": gain(ks[19], (N_ATTN_LAYERS, ATTN_V_DIM)),
        "attn_out_w": nrm(ks[20], (N_ATTN_LAYERS, D_MODEL, D_MODEL), D_MODEL ** -0.5 * out_scale),
        "ffn_up_w": nrm(ks[21], (DEPTH, D_MODEL, 2 * D_FF), D_MODEL ** -0.5),
        "ffn_conv_w": nrm(ks[22], (DEPTH, FFN_CONV, 2 * D_FF), FFN_CONV ** -0.5),
        "ffn_conv_b": nrm(ks[23], (DEPTH, 2 * D_FF), 0.02),
        "ffn_down_w": nrm(ks[24], (DEPTH, D_FF, D_MODEL), D_FF ** -0.5 * out_scale),
    }


def reference(x, positions, norm_mix_g, norm_ffn_g,
              ssd_in_w, ssd_conv_w, ssd_conv_b, ssd_dt_bias, ssd_a_log, ssd_d,
              ssd_norm_g, ssd_out_w,
              attn_in_w, attn_q_norm_g, attn_k_norm_g, attn_lq1, attn_lk1,
              attn_lq2, attn_lk2, attn_subln_g, attn_out_w,
              ffn_up_w, ffn_conv_w, ffn_conv_b, ffn_down_w):
    cos, sin = rope_tables(positions)
    for i in range(DEPTH):
        j = i // N_MIXERS
        h = rmsnorm(x, norm_mix_g[i])
        if i % N_MIXERS == 0:
            x = x + ssd_mixer(h, ssd_in_w[j], ssd_conv_w[j], ssd_conv_b[j],
                              ssd_dt_bias[j], ssd_a_log[j], ssd_d[j],
                              ssd_norm_g[j], ssd_out_w[j])
        else:
            lambda_init = 0.8 - 0.6 * math.exp(-0.3 * i)
            x = x + diff_attention(h, cos, sin, attn_in_w[j], attn_q_norm_g[j],
                                   attn_k_norm_g[j], attn_lq1[j], attn_lk1[j],
                                   attn_lq2[j], attn_lk2[j], attn_subln_g[j],
                                   attn_out_w[j], lambda_init)
        h = rmsnorm(x, norm_ffn_g[i])
        x = x + conv_ffn(h, ffn_up_w[i], ffn_conv_w[i], ffn_conv_b[i], ffn_down_w[i])
    return x
```

```python
import functools
import math

import jax
import jax.numpy as jnp
from jax import lax
from jax.experimental import pallas as pl
from jax.experimental.pallas import tpu as pltpu

F32 = jnp.float32
BF16 = jnp.bfloat16

V7X_LANES = 128
V7X_SUBLANES = 8
V7X_VMEM_LIMIT_BYTES = 56 * 1024 * 1024

D_MODEL = 1024
CHUNK = 64
EPS = 1e-6
SSD_D_INNER = 2048
SSD_HEAD_DIM = 64
SSD_HEADS = 32
SSD_GROUPS = 4
SSD_HEADS_PER_GROUP = 8
SSD_STATE = 128
SSD_CONV = 4
SSD_CONV_CH = SSD_D_INNER + 2 * SSD_GROUPS * SSD_STATE
ATTN_HEADS = 8
ATTN_HEAD_DIM = 64
ATTN_V_DIM = 128
ROPE_THETA = 500000.0
ROPE_DIM = 16
D_FF = 2816
FFN_CONV = 3
N_MIXERS = 2

ROW_TILE = 512
SSD_SCAN_CHUNK = 128
SSD_CONV_COLS = 512
FFN_COLS = 256
ATTN_TILE = 256
K_PREP_ROWS = 256


def _dot(a, b):
    return jnp.dot(a, b, preferred_element_type=F32)


def _dot_nt(a, b):
    return lax.dot_general(a, b, (((1,), (1,)), ((), ())), preferred_element_type=F32)


def _dot_tn(a, b):
    return lax.dot_general(a, b, (((0,), (0,)), ((), ())), preferred_element_type=F32)


def _sigmoid(x):
    return 1.0 / (1.0 + jnp.exp(-x))


def _rmsnorm(x, g):
    return x * lax.rsqrt(jnp.mean(x * x, axis=-1, keepdims=True) + EPS) * g


def _causal_conv(a, prev, w, b):
    kw = w.shape[0]
    top = a[:V7X_SUBLANES]
    ext = jnp.concatenate([prev, top], axis=0)
    y = b + w[kw - 1:kw] * a
    y_top = b + w[kw - 1:kw] * top
    for j in range(1, kw):
        wj = w[kw - 1 - j:kw - j]
        y = y + wj * pltpu.roll(a, j, axis=0)
        y_top = y_top + wj * pltpu.roll(ext, j, axis=0)[V7X_SUBLANES:]
    return jnp.concatenate([y_top, y[V7X_SUBLANES:]], axis=0)


def _const_spec(shape):
    nd = len(shape)
    return pl.BlockSpec(shape, lambda *_: (0,) * nd, pipeline_mode=pl.Buffered(1))


def _params(semantics):
    return pltpu.CompilerParams(dimension_semantics=semantics,
                                vmem_limit_bytes=V7X_VMEM_LIMIT_BYTES)


def _rope_kernel(pos_ref, freq_ref, m1_ref, m2_ref, c_ref, s1_ref, s2_ref):
    ang = pos_ref[...].astype(F32) * freq_ref[...]
    s = jnp.sin(ang)
    c_ref[...] = jnp.cos(ang)
    s1_ref[...] = s * m1_ref[...]
    s2_ref[...] = s * m2_ref[...]


def _rope_tables(positions):
    b, s = positions.shape
    m = b * s
    inv_freq = ROPE_THETA ** (-jnp.arange(0, ROPE_DIM, 2, dtype=F32) / ROPE_DIM)
    d = jnp.arange(V7X_LANES) % ATTN_HEAD_DIM
    half = ROPE_DIM // 2
    freq = jnp.where(d < ROPE_DIM, inv_freq[d % half], 0.0).astype(F32)[None, :]
    m1 = jnp.where(d < half, -1.0, 0.0).astype(F32)[None, :]
    m2 = jnp.where((d >= half) & (d < ROPE_DIM), 1.0, 0.0).astype(F32)[None, :]
    pos = jnp.broadcast_to(positions.reshape(m, 1), (m, V7X_LANES))
    row = pl.BlockSpec((s, V7X_LANES), lambda i: (i, 0))
    vec = pl.BlockSpec((1, V7X_LANES), lambda i: (0, 0))
    out = jax.ShapeDtypeStruct((m, V7X_LANES), F32)
    return pl.pallas_call(
        _rope_kernel, grid=(b,), in_specs=[row, vec, vec, vec], out_specs=[row, row, row],
        out_shape=[out, out, out], compiler_params=_params(("arbitrary",)), name="rope_tables",
    )(pos, freq, m1, m2)


def _ssd_in_kernel(x_ref, g_ref, wz_ref, wxbc_ref, wdt_ref, cw_ref, cb_ref, dtb_ref,
                   z_ref, xs_ref, b_ref, c_ref, dt_ref, carry_ref, *, tiles_per_seq):
    tm = x_ref.shape[0]

    @pl.when(pl.program_id(0) % tiles_per_seq == 0)
    def _():
        carry_ref[...] = jnp.zeros_like(carry_ref)

    h = _rmsnorm(x_ref[...], g_ref[...]).astype(BF16)
    z_ref[...] = _dot(h, wz_ref[...]).astype(BF16)
    dt = _dot(h, wdt_ref[...]) + dtb_ref[...]
    dt_ref[...] = jnp.maximum(dt, 0.0) + jnp.log1p(jnp.exp(-jnp.abs(dt)))
    gn = SSD_GROUPS * SSD_STATE
    for c in range(SSD_CONV_CH // SSD_CONV_COLS):
        c0 = c * SSD_CONV_COLS
        cols = slice(c0, c0 + SSD_CONV_COLS)
        a = _dot(h, wxbc_ref[:, cols])
        prev = carry_ref[:, cols]
        carry_ref[:, cols] = a[tm - V7X_SUBLANES:]
        y = _causal_conv(a, prev, cw_ref[:, cols], cb_ref[:, cols])
        y = (y * _sigmoid(y)).astype(BF16)
        if c0 < SSD_D_INNER:
            xs_ref[:, cols] = y
        elif c0 < SSD_D_INNER + gn:
            b_ref[:, c0 - SSD_D_INNER:c0 - SSD_D_INNER + SSD_CONV_COLS] = y
        else:
            c_ref[:, c0 - SSD_D_INNER - gn:c0 - SSD_D_INNER - gn + SSD_CONV_COLS] = y


def _ssd_core_kernel(xs_ref, b_ref, c_ref, dt_ref, z_ref, x_ref, alog_ref, dskip_ref, e_ref,
                     ng_ref, wo_ref, o_ref, h_ref, y_ref):
    ts = xs_ref.shape[0]
    L = SSD_SCAN_CHUNK
    P2 = 2 * SSD_HEAD_DIM
    GP = SSD_HEADS_PER_GROUP * SSD_HEAD_DIM

    @pl.when(pl.program_id(1) == 0)
    def _():
        h_ref[...] = jnp.zeros_like(h_ref)

    row = lax.broadcasted_iota(jnp.int32, (L, L), 0)
    col = lax.broadcasted_iota(jnp.int32, (L, L), 1)
    causal = row >= col
    tril = jnp.where(causal, 1.0, 0.0).astype(BF16)
    left = lax.broadcasted_iota(jnp.int32, (L, P2), 1) < SSD_HEAD_DIM
    a_row = -jnp.exp(alog_ref[...])

    def chunk_body(ci, carry):
        rows = pl.ds(pl.multiple_of(ci * L, L), L)
        dt = dt_ref[rows, :]
        da = dt * a_row
        hi = da.astype(BF16)
        r1 = da - hi.astype(F32)
        mid = r1.astype(BF16)
        lo = (r1 - mid.astype(F32)).astype(BF16)
        cs = _dot(tril, hi) + _dot(tril, mid) + _dot(tril, lo)
        cs_t = cs.T
        ecs = jnp.exp(cs)
        dec = jnp.exp(cs[L - 1:L, :] - cs)
        stacked = jnp.concatenate([dt, dec, ecs], axis=0).astype(BF16)
        wide = _dot(stacked, e_ref[...])
        dt_e, dec_e, ecs_e = wide[:L], wide[L:2 * L], wide[2 * L:]
        xs = xs_ref[rows, :].astype(F32)
        xdt = xs * dt_e
        xdt_b = xdt.astype(BF16)
        dx_b = (xdt * dec_e).astype(BF16)
        for g in range(SSD_GROUPS):
            gc = slice(g * SSD_STATE, (g + 1) * SSD_STATE)
            hc = slice(g * GP, (g + 1) * GP)
            bg = b_ref[rows, gc]
            cg = c_ref[rows, gc]
            cb = _dot_nt(cg, bg)
            hst = h_ref[g]
            y_off = _dot(cg, hst.astype(BF16)) * ecs_e[:, hc]
            h_ref[g] = hst * ecs_e[L - 1:L, hc] + _dot_tn(bg, dx_b[:, hc])
            for p in range(SSD_HEADS_PER_GROUP // 2):
                h0 = g * SSD_HEADS_PER_GROUP + 2 * p
                pc = slice(h0 * SSD_HEAD_DIM, h0 * SSD_HEAD_DIM + P2)
                rhs = xdt_b[:, pc]

                def lmat(hd):
                    seg = cs[:, hd:hd + 1] - cs_t[hd:hd + 1, :]
                    return (cb * jnp.exp(jnp.where(causal, seg, -jnp.inf))).astype(BF16)

                yd = jnp.where(left, _dot(lmat(h0), rhs), _dot(lmat(h0 + 1), rhs))
                y_ref[rows, pc] = yd + y_off[:, p * P2:(p + 1) * P2] + xs[:, pc] * dskip_ref[:, pc]
        return carry

    lax.fori_loop(0, ts // L, chunk_body, 0)

    z = z_ref[...].astype(F32)
    gated = y_ref[...] * (z * _sigmoid(z))
    gw = SSD_D_INNER // SSD_GROUPS
    parts = []
    for g in range(SSD_GROUPS):
        gg = gated[:, g * gw:(g + 1) * gw]
        parts.append(gg * lax.rsqrt(jnp.mean(gg * gg, axis=-1, keepdims=True) + EPS))
    yn = (jnp.concatenate(parts, axis=-1) * ng_ref[...]).astype(BF16)
    o_ref[...] = x_ref[...] + _dot(yn, wo_ref[...])


def _ssd_layer(x2, batch, seq, norm_g, in_w, conv_w, conv_b, dt_bias, a_log, d_skip, ssd_norm_g, out_w):
    m = x2.shape[0]
    tm = ROW_TILE
    gn = SSD_GROUPS * SSD_STATE
    wz = in_w[:, :SSD_D_INNER].astype(BF16)
    wxbc = in_w[:, SSD_D_INNER:SSD_D_INNER + SSD_CONV_CH].astype(BF16)
    pad = V7X_LANES - SSD_HEADS
    wdt = jnp.pad(in_w[:, SSD_D_INNER + SSD_CONV_CH:], ((0, 0), (0, pad))).astype(BF16)
    dtb = jnp.pad(dt_bias, (0, pad))[None, :]
    alog = jnp.pad(a_log, (0, pad))[None, :]

    def rows(width):
        return pl.BlockSpec((tm, width), lambda i: (i, 0))

    z, xs, bm, cm, dt = pl.pallas_call(
        functools.partial(_ssd_in_kernel, tiles_per_seq=seq // tm),
        grid=(m // tm,),
        in_specs=[rows(D_MODEL), _const_spec((1, D_MODEL)), _const_spec(wz.shape), _const_spec(wxbc.shape),
                  _const_spec(wdt.shape), _const_spec(conv_w.shape), _const_spec((1, SSD_CONV_CH)),
                  _const_spec((1, V7X_LANES))],
        out_specs=[rows(SSD_D_INNER), rows(SSD_D_INNER), rows(gn), rows(gn), rows(V7X_LANES)],
        out_shape=[jax.ShapeDtypeStruct((m, SSD_D_INNER), BF16), jax.ShapeDtypeStruct((m, SSD_D_INNER), BF16),
                   jax.ShapeDtypeStruct((m, gn), BF16), jax.ShapeDtypeStruct((m, gn), BF16),
                   jax.ShapeDtypeStruct((m, V7X_LANES), F32)],
        scratch_shapes=[pltpu.VMEM((V7X_SUBLANES, SSD_CONV_CH), F32)],
        compiler_params=_params(("arbitrary",)), name="ssd_in",
    )(x2, norm_g[None, :], wz, wxbc, wdt, conv_w, conv_b[None, :], dtb)

    expand = (jnp.arange(V7X_LANES)[:, None] == (jnp.arange(SSD_D_INNER) // SSD_HEAD_DIM)[None, :]).astype(BF16)
    dskip = jnp.repeat(d_skip, SSD_HEAD_DIM)[None, :]
    ts = ROW_TILE
    spt = seq // ts

    def tile(width):
        return pl.BlockSpec((ts, width), lambda b, s: (b * spt + s, 0))

    return pl.pallas_call(
        _ssd_core_kernel,
        grid=(batch, spt),
        in_specs=[tile(SSD_D_INNER), tile(gn), tile(gn), tile(V7X_LANES), tile(SSD_D_INNER), tile(D_MODEL),
                  _const_spec((1, V7X_LANES)), _const_spec((1, SSD_D_INNER)), _const_spec(expand.shape),
                  _const_spec((1, SSD_D_INNER)), _const_spec((SSD_D_INNER, D_MODEL))],
        out_specs=tile(D_MODEL),
        out_shape=jax.ShapeDtypeStruct((m, D_MODEL), F32),
        scratch_shapes=[pltpu.VMEM((SSD_GROUPS, SSD_STATE, SSD_HEADS_PER_GROUP * SSD_HEAD_DIM), F32),
                        pltpu.VMEM((ts, SSD_D_INNER), F32)],
        compiler_params=_params(("arbitrary", "arbitrary")), name="ssd_core",
    )(xs, bm, cm, dt, z, x2, alog, dskip, expand, ssd_norm_g[None, :], out_w.astype(BF16))


def _norm_matmul_kernel(x_ref, g_ref, w_ref, o_ref, *, col_chunk):
    h = _rmsnorm(x_ref[...], g_ref[...]).astype(BF16)
    for c0 in range(0, w_ref.shape[1], col_chunk):
        cols = slice(c0, c0 + col_chunk)
        o_ref[:, cols] = _dot(h, w_ref[:, cols]).astype(o_ref.dtype)


def _matmul_residual_kernel(x_ref, a_ref, w_ref, o_ref):
    o_ref[...] = x_ref[...] + _dot(a_ref[...], w_ref[...])


def _flash_kernel(q_ref, k_ref, v_ref, c_ref, s1_ref, s2_ref, qg_ref, kg_ref,
                  lq1_ref, lk1_ref, lq2_ref, lk2_ref, sg_ref, o_ref, kh_ref, *, lambda_init):
    t = ATTN_TILE
    seq = k_ref.shape[0]
    qi = pl.program_id(2)
    left_row = lax.broadcasted_iota(jnp.int32, (1, V7X_LANES), 1) < ATTN_HEAD_DIM

    def norm_rope(x, r0, nrows, g):
        rows = pl.ds(r0, nrows)
        x2 = x * x
        ss0 = jnp.sum(jnp.where(left_row, x2, 0.0), axis=-1, keepdims=True)
        ss1 = jnp.sum(jnp.where(left_row, 0.0, x2), axis=-1, keepdims=True)
        ms = jnp.where(left_row, ss0, ss1) * (1.0 / ATTN_HEAD_DIM)
        xn = x * lax.rsqrt(ms + EPS) * g
        half = ROPE_DIM // 2
        return (xn * c_ref[rows, :] + pltpu.roll(xn, V7X_LANES - half, axis=1) * s1_ref[rows, :]
                + pltpu.roll(xn, half, axis=1) * s2_ref[rows, :])

    @pl.when(qi == 0)
    def _():
        def prep(j, carry):
            r0 = pl.multiple_of(j * K_PREP_ROWS, K_PREP_ROWS)
            kx = k_ref[pl.ds(r0, K_PREP_ROWS), :].astype(F32)
            kh_ref[pl.ds(r0, K_PREP_ROWS), :] = norm_rope(kx, r0, K_PREP_ROWS, kg_ref[...]).astype(BF16)
            return carry
        lax.fori_loop(0, seq // K_PREP_ROWS, prep, 0)

    q0row = pl.multiple_of(qi * t, t)
    scale = 1.0 / math.sqrt(ATTN_HEAD_DIM)
    qh = norm_rope(q_ref[...].astype(F32), q0row, t, qg_ref[...]) * scale
    qa = jnp.where(left_row, qh, 0.0).astype(BF16)
    qb = jnp.where(left_row, 0.0, qh).astype(BF16)

    def softmax_step(s, m, l):
        mn = jnp.maximum(m, jnp.max(s, axis=-1, keepdims=True))
        p = jnp.exp(s - mn)
        alpha = jnp.exp(m - mn)
        return p, alpha, mn, alpha * l + jnp.sum(p, axis=-1, keepdims=True)

    def kv_step(kt, carry, diagonal):
        ma, la, acc_a, mb, lb, acc_b = carry
        rows = pl.ds(pl.multiple_of(kt * t, t), t)
        ks = kh_ref[rows, :]
        sa = _dot_nt(qa, ks)
        sb = _dot_nt(qb, ks)
        if diagonal:
            rc = lax.broadcasted_iota(jnp.int32, (t, t), 0) // CHUNK
            cc = lax.broadcasted_iota(jnp.int32, (t, t), 1) // CHUNK
            allowed = cc <= rc
            sa = jnp.where(allowed, sa, -jnp.inf)
            sb = jnp.where(allowed, sb, -jnp.inf)
        pa, alpha_a, ma, la = softmax_step(sa, ma, la)
        pb, alpha_b, mb, lb = softmax_step(sb, mb, lb)
        pv = _dot(jnp.concatenate([pa, pb], axis=0).astype(BF16), v_ref[rows, :])
        return ma, la, alpha_a * acc_a + pv[:t], mb, lb, alpha_b * acc_b + pv[t:]

    neg = jnp.full((t, 1), -jnp.inf, F32)
    zero1 = jnp.zeros((t, 1), F32)
    zero = jnp.zeros((t, ATTN_V_DIM), F32)
    carry = lax.fori_loop(0, qi, functools.partial(kv_step, diagonal=False),
                          (neg, zero1, zero, neg, zero1, zero))
    _, la, acc_a, _, lb, acc_b = kv_step(qi, carry, True)

    lam = (jnp.exp(jnp.sum(lq1_ref[...] * lk1_ref[...], axis=-1, keepdims=True))
           - jnp.exp(jnp.sum(lq2_ref[...] * lk2_ref[...], axis=-1, keepdims=True)) + lambda_init)
    o = acc_a / la - lam * (acc_b / lb)
    o = _rmsnorm(o, sg_ref[...]) * (1.0 - lambda_init)
    o_ref[...] = o.astype(o_ref.dtype)


def _attn_layer(x2, batch, seq, tables, norm_g, in_w, q_norm_g, k_norm_g, lq1, lk1, lq2, lk2,
                subln_g, out_w, lambda_init):
    m = x2.shape[0]
    tm = ROW_TILE
    n_in = in_w.shape[1]
    qkv = pl.pallas_call(
        functools.partial(_norm_matmul_kernel, col_chunk=D_MODEL),
        grid=(m // tm,),
        in_specs=[pl.BlockSpec((tm, D_MODEL), lambda i: (i, 0)), _const_spec((1, D_MODEL)),
                  _const_spec((D_MODEL, n_in))],
        out_specs=pl.BlockSpec((tm, n_in), lambda i: (i, 0)),
        out_shape=jax.ShapeDtypeStruct((m, n_in), BF16),
        compiler_params=_params(("arbitrary",)), name="attn_in",
    )(x2, norm_g[None, :], in_w.astype(BF16))

    t = ATTN_TILE
    qt = seq // t
    hw = 2 * ATTN_HEAD_DIM
    cos_t, sin1_t, sin2_t = tables
    seq_rows = pl.BlockSpec((seq, V7X_LANES), lambda b, h, q: (b, 0))
    vec = pl.BlockSpec((1, hw), lambda b, h, q: (0, 0))
    vec64 = pl.BlockSpec((1, ATTN_HEAD_DIM), lambda b, h, q: (0, 0))
    o = pl.pallas_call(
        functools.partial(_flash_kernel, lambda_init=lambda_init),
        grid=(batch, ATTN_HEADS, qt),
        in_specs=[pl.BlockSpec((t, hw), lambda b, h, q: (b * qt + q, h)),
                  pl.BlockSpec((seq, hw), lambda b, h, q: (b, ATTN_HEADS + h)),
                  pl.BlockSpec((seq, hw), lambda b, h, q: (b, 2 * ATTN_HEADS + h)),
                  seq_rows, seq_rows, seq_rows, vec, vec, vec64, vec64, vec64, vec64, vec],
        out_specs=pl.BlockSpec((t, hw), lambda b, h, q: (b * qt + q, h)),
        out_shape=jax.ShapeDtypeStruct((m, D_MODEL), BF16),
        scratch_shapes=[pltpu.VMEM((seq, hw), BF16)],
        compiler_params=_params(("arbitrary", "arbitrary", "arbitrary")), name="flash",
    )(qkv, qkv, qkv, cos_t, sin1_t, sin2_t, jnp.tile(q_norm_g, 2)[None, :], jnp.tile(k_norm_g, 2)[None, :],
      lq1[None, :], lk1[None, :], lq2[None, :], lk2[None, :], subln_g[None, :])

    return pl.pallas_call(
        _matmul_residual_kernel,
        grid=(m // tm,),
        in_specs=[pl.BlockSpec((tm, D_MODEL), lambda i: (i, 0)), pl.BlockSpec((tm, D_MODEL), lambda i: (i, 0)),
                  _const_spec((D_MODEL, D_MODEL))],
        out_specs=pl.BlockSpec((tm, D_MODEL), lambda i: (i, 0)),
        out_shape=jax.ShapeDtypeStruct((m, D_MODEL), F32),
        compiler_params=_params(("arbitrary",)), name="attn_out",
    )(x2, o, out_w.astype(BF16))


def _ffn_kernel(x_ref, g_ref, wg_ref, wu_ref, cwg_ref, cwu_ref, cbg_ref, cbu_ref, wd_ref,
                o_ref, carry_g_ref, carry_u_ref, act_ref, *, tiles_per_seq):
    tm = x_ref.shape[0]

    @pl.when(pl.program_id(0) % tiles_per_seq == 0)
    def _():
        carry_g_ref[...] = jnp.zeros_like(carry_g_ref)
        carry_u_ref[...] = jnp.zeros_like(carry_u_ref)

    x = x_ref[...]
    h = _rmsnorm(x, g_ref[...]).astype(BF16)
    for c0 in range(0, D_FF, FFN_COLS):
        cols = slice(c0, c0 + FFN_COLS)
        ag = _dot(h, wg_ref[:, cols])
        au = _dot(h, wu_ref[:, cols])
        pg = carry_g_ref[:, cols]
        pu = carry_u_ref[:, cols]
        carry_g_ref[:, cols] = ag[tm - V7X_SUBLANES:]
        carry_u_ref[:, cols] = au[tm - V7X_SUBLANES:]
        yg = _causal_conv(ag, pg, cwg_ref[:, cols], cbg_ref[:, cols])
        yu = _causal_conv(au, pu, cwu_ref[:, cols], cbu_ref[:, cols])
        act_ref[:, cols] = (yg * _sigmoid(yg) * yu).astype(BF16)
    o_ref[...] = x + _dot(act_ref[...], wd_ref[...])


def _ffn_layer(x2, seq, norm_g, up_w, conv_w, conv_b, down_w):
    m = x2.shape[0]
    tm = ROW_TILE
    wg = up_w[:, :D_FF].astype(BF16)
    wu = up_w[:, D_FF:].astype(BF16)
    row = pl.BlockSpec((tm, D_MODEL), lambda i: (i, 0))
    return pl.pallas_call(
        functools.partial(_ffn_kernel, tiles_per_seq=seq // tm),
        grid=(m // tm,),
        in_specs=[row, _const_spec((1, D_MODEL)), _const_spec((D_MODEL, D_FF)), _const_spec((D_MODEL, D_FF)),
                  _const_spec((FFN_CONV, D_FF)), _const_spec((FFN_CONV, D_FF)), _const_spec((1, D_FF)),
                  _const_spec((1, D_FF)), _const_spec((D_FF, D_MODEL))],
        out_specs=row,
        out_shape=jax.ShapeDtypeStruct((m, D_MODEL), F32),
        scratch_shapes=[pltpu.VMEM((V7X_SUBLANES, D_FF), F32), pltpu.VMEM((V7X_SUBLANES, D_FF), F32),
                        pltpu.VMEM((tm, D_FF), BF16)],
        compiler_params=_params(("arbitrary",)), name="ffn",
    )(x2, norm_g[None, :], wg, wu, conv_w[:, :D_FF], conv_w[:, D_FF:], conv_b[None, :D_FF],
      conv_b[None, D_FF:], down_w.astype(BF16))


def kernel(x, positions, norm_mix_g, norm_ffn_g, ssd_in_w, ssd_conv_w, ssd_conv_b, ssd_dt_bias, ssd_a_log, ssd_d, ssd_norm_g, ssd_out_w, attn_in_w, attn_q_norm_g, attn_k_norm_g, attn_lq1, attn_lk1, attn_lq2, attn_lk2, attn_subln_g, attn_out_w, ffn_up_w, ffn_conv_w, ffn_conv_b, ffn_down_w):
    batch, seq, d = x.shape
    depth = norm_mix_g.shape[0]
    assert d == D_MODEL and seq % ROW_TILE == 0 and seq % ATTN_TILE == 0
    x2 = x.reshape(batch * seq, d)
    tables = _rope_tables(positions)
    for i in range(depth):
        j = i // N_MIXERS
        if i % N_MIXERS == 0:
            x2 = _ssd_layer(x2, batch, seq, norm_mix_g[i], ssd_in_w[j], ssd_conv_w[j], ssd_conv_b[j],
                            ssd_dt_bias[j], ssd_a_log[j], ssd_d[j], ssd_norm_g[j], ssd_out_w[j])
        else:
            lambda_init = 0.8 - 0.6 * math.exp(-0.3 * i)
            x2 = _attn_layer(x2, batch, seq, tables, norm_mix_g[i], attn_in_w[j], attn_q_norm_g[j],
                             attn_k_norm_g[j], attn_lq1[j], attn_lk1[j], attn_lq2[j], attn_lk2[j],
                             attn_subln_g[j], attn_out_w[j], lambda_init)
        x2 = _ffn_layer(x2, seq, norm_ffn_g[i], ffn_up_w[i], ffn_conv_w[i], ffn_conv_b[i], ffn_down_w[i])
    return x2.reshape(batch, seq, d)
```

```python
import functools
import math

import jax
import jax.numpy as jnp
from jax import lax
from jax.experimental import pallas as pl
from jax.experimental.pallas import tpu as pltpu

F32 = jnp.float32
BF16 = jnp.bfloat16

V7X_LANES = 128
V7X_SUBLANES = 8
V7X_BF16_SUBLANES = 16
V7X_VMEM_LIMIT_BYTES = 56 * 1024 * 1024

D_MODEL = 1024
CHUNK = 64
EPS = 1e-6
SSD_D_INNER = 2048
SSD_HEAD_DIM = 64
SSD_HEADS = 32
SSD_GROUPS = 4
SSD_HEADS_PER_GROUP = 8
SSD_STATE = 128
SSD_CONV = 4
SSD_CONV_CH = SSD_D_INNER + 2 * SSD_GROUPS * SSD_STATE
ATTN_HEADS = 8
ATTN_HEAD_DIM = 64
ATTN_V_DIM = 128
ATTN_V_ROWS = ATTN_V_DIM + V7X_BF16_SUBLANES
LOG2_E = math.log2(math.e)
ROPE_THETA = 500000.0
ROPE_DIM = 16
D_FF = 2816
FFN_CONV = 3
N_MIXERS = 2

ROW_TILE = 512
SSD_SCAN_CHUNK = 128
SSD_CONV_COLS = 512
FFN_COLS = 256
ATTN_TILE = 256
K_PREP_ROWS = 256
ATTN_HEADS_PER_STEP = 8


def _dot(a, b):
    return jnp.dot(a, b, preferred_element_type=F32)


def _dot_nt(a, b):
    return lax.dot_general(a, b, (((1,), (1,)), ((), ())), preferred_element_type=F32)


def _dot_tn(a, b):
    return lax.dot_general(a, b, (((0,), (0,)), ((), ())), preferred_element_type=F32)


def _silu(x):
    h = 0.5 * x
    return h + h * jnp.tanh(h)


def _rmsnorm(x, g):
    return x * lax.rsqrt(jnp.mean(x * x, axis=-1, keepdims=True) + EPS) * g


def _conv_store(buf_ref, c0, a):
    for j in range(a.shape[1] // V7X_LANES):
        buf_ref[c0 // V7X_LANES + j, V7X_SUBLANES:, :] = a[:, j * V7X_LANES:(j + 1) * V7X_LANES]


def _causal_conv(buf_ref, slab, w, b):
    kw = w.shape[0]
    tm = buf_ref.shape[1] - V7X_SUBLANES
    y = b + w[kw - 1:kw] * buf_ref[slab, V7X_SUBLANES:, :]
    for j in range(1, kw):
        y = y + w[kw - 1 - j:kw - j] * buf_ref[slab, V7X_SUBLANES - j:V7X_SUBLANES - j + tm, :]
    return y


def _conv_history_reset(buf_ref, first):
    @pl.when(first)
    def _():
        buf_ref[:, :V7X_SUBLANES, :] = jnp.zeros((buf_ref.shape[0], V7X_SUBLANES, V7X_LANES), buf_ref.dtype)


def _conv_history_carry(buf_ref):
    tm = buf_ref.shape[1] - V7X_SUBLANES
    buf_ref[:, :V7X_SUBLANES, :] = buf_ref[:, tm:, :]


def _const_spec(shape):
    nd = len(shape)
    return pl.BlockSpec(shape, lambda *_: (0,) * nd, pipeline_mode=pl.Buffered(1))


def _params(semantics):
    return pltpu.CompilerParams(dimension_semantics=semantics,
                                vmem_limit_bytes=V7X_VMEM_LIMIT_BYTES)


def _rope_kernel(pos_ref, freq_ref, m1_ref, m2_ref, c_ref, s1_ref, s2_ref):
    ang = pos_ref[...].astype(F32) * freq_ref[...]
    s = jnp.sin(ang)
    c_ref[...] = jnp.cos(ang)
    s1_ref[...] = s * m1_ref[...]
    s2_ref[...] = s * m2_ref[...]


def _rope_tables(positions):
    b, s = positions.shape
    m = b * s
    inv_freq = ROPE_THETA ** (-jnp.arange(0, ROPE_DIM, 2, dtype=F32) / ROPE_DIM)
    d = jnp.arange(V7X_LANES) % ATTN_HEAD_DIM
    half = ROPE_DIM // 2
    freq = jnp.where(d < ROPE_DIM, inv_freq[d % half], 0.0).astype(F32)[None, :]
    m1 = jnp.where(d < half, -1.0, 0.0).astype(F32)[None, :]
    m2 = jnp.where((d >= half) & (d < ROPE_DIM), 1.0, 0.0).astype(F32)[None, :]
    pos = jnp.broadcast_to(positions.reshape(m, 1), (m, V7X_LANES))
    row = pl.BlockSpec((s, V7X_LANES), lambda i: (i, 0))
    vec = pl.BlockSpec((1, V7X_LANES), lambda i: (0, 0))
    out = jax.ShapeDtypeStruct((m, V7X_LANES), F32)
    return pl.pallas_call(
        _rope_kernel, grid=(b,), in_specs=[row, vec, vec, vec], out_specs=[row, row, row],
        out_shape=[out, out, out], compiler_params=_params(("arbitrary",)), name="rope_tables",
    )(pos, freq, m1, m2)


def _ssd_in_kernel(x_ref, g_ref, wz_ref, wxbc_ref, wdt_ref, cw_ref, cb_ref, dtb_ref,
                   z_ref, xs_ref, b_ref, c_ref, dt_ref, buf_ref, *, tiles_per_seq):
    tm = x_ref.shape[0]
    _conv_history_reset(buf_ref, pl.program_id(0) % tiles_per_seq == 0)
    h = _rmsnorm(x_ref[...], g_ref[...]).astype(BF16)
    z_ref[...] = _dot(h, wz_ref[...]).astype(BF16)
    dt = _dot(h, wdt_ref[...]) + dtb_ref[...]
    dt_ref[...] = jnp.maximum(dt, 0.0) + jnp.log1p(jnp.exp(-jnp.abs(dt)))
    gn = SSD_GROUPS * SSD_STATE
    for c0 in range(0, SSD_CONV_CH, SSD_CONV_COLS):
        _conv_store(buf_ref, c0, _dot(h, wxbc_ref[:, c0:c0 + SSD_CONV_COLS]))
        for l0 in range(c0, c0 + SSD_CONV_COLS, V7X_LANES):
            lanes = slice(l0, l0 + V7X_LANES)
            y = _silu(_causal_conv(buf_ref, l0 // V7X_LANES, cw_ref[:, lanes], cb_ref[:, lanes])).astype(BF16)
            if l0 < SSD_D_INNER:
                xs_ref[:, lanes] = y
            elif l0 < SSD_D_INNER + gn:
                b_ref[:, l0 - SSD_D_INNER:l0 - SSD_D_INNER + V7X_LANES] = y
            else:
                c_ref[:, l0 - SSD_D_INNER - gn:l0 - SSD_D_INNER - gn + V7X_LANES] = y
    _conv_history_carry(buf_ref)


def _ssd_core_kernel(xs_ref, b_ref, c_ref, dt_ref, z_ref, x_ref, alog_ref, dskip_ref, e_ref,
                     ng_ref, wo_ref, o_ref, h_ref, y_ref):
    ts = xs_ref.shape[0]
    L = SSD_SCAN_CHUNK
    P2 = 2 * SSD_HEAD_DIM
    GP = SSD_HEADS_PER_GROUP * SSD_HEAD_DIM

    @pl.when(pl.program_id(1) == 0)
    def _():
        h_ref[...] = jnp.zeros_like(h_ref)

    row = lax.broadcasted_iota(jnp.int32, (L, L), 0)
    col = lax.broadcasted_iota(jnp.int32, (L, L), 1)
    causal = row >= col
    tril = jnp.where(causal, 1.0, 0.0).astype(BF16)
    left = lax.broadcasted_iota(jnp.int32, (L, P2), 1) < SSD_HEAD_DIM
    a_row = -jnp.exp(alog_ref[...])

    def chunk_body(ci, carry):
        rows = pl.ds(pl.multiple_of(ci * L, L), L)
        dt = dt_ref[rows, :]
        da = dt * a_row
        hi = da.astype(BF16)
        r1 = da - hi.astype(F32)
        mid = r1.astype(BF16)
        lo = (r1 - mid.astype(F32)).astype(BF16)
        cs = _dot(tril, hi) + _dot(tril, mid) + _dot(tril, lo)
        cs_t = cs.T
        ecs = jnp.exp(cs)
        dec = jnp.exp(cs[L - 1:L, :] - cs)
        stacked = jnp.concatenate([dt, dec, ecs], axis=0).astype(BF16)
        wide = _dot(stacked, e_ref[...])
        dt_e, dec_e, ecs_e = wide[:L], wide[L:2 * L], wide[2 * L:]
        xs = xs_ref[rows, :].astype(F32)
        xdt = xs * dt_e
        xdt_b = xdt.astype(BF16)
        dx_b = (xdt * dec_e).astype(BF16)
        for g in range(SSD_GROUPS):
            gc = slice(g * SSD_STATE, (g + 1) * SSD_STATE)
            hc = slice(g * GP, (g + 1) * GP)
            bg = b_ref[rows, gc]
            cg = c_ref[rows, gc]
            cb = _dot_nt(cg, bg)
            hst = h_ref[g]
            y_off = _dot(cg, hst.astype(BF16)) * ecs_e[:, hc]
            h_ref[g] = hst * ecs_e[L - 1:L, hc] + _dot_tn(bg, dx_b[:, hc])
            for p in range(SSD_HEADS_PER_GROUP // 2):
                h0 = g * SSD_HEADS_PER_GROUP + 2 * p
                pc = slice(h0 * SSD_HEAD_DIM, h0 * SSD_HEAD_DIM + P2)
                pair = xdt_b[:, pc]

                def lmat(hd):
                    seg = cs[:, hd:hd + 1] - cs_t[hd:hd + 1, :]
                    return (cb * jnp.exp(jnp.where(causal, seg, -jnp.inf))).astype(BF16)

                lhs = jnp.concatenate([lmat(h0), lmat(h0 + 1)], axis=1)
                rhs = jnp.concatenate([jnp.where(left, pair, 0), jnp.where(left, 0, pair)], axis=0)
                y_ref[rows, pc] = (_dot(lhs, rhs) + y_off[:, p * P2:(p + 1) * P2]
                                   + xs[:, pc] * dskip_ref[:, pc])
        return carry

    lax.fori_loop(0, ts // L, chunk_body, 0)

    gated = y_ref[...] * _silu(z_ref[...].astype(F32))
    gw = SSD_D_INNER // SSD_GROUPS
    parts = []
    for g in range(SSD_GROUPS):
        gg = gated[:, g * gw:(g + 1) * gw]
        parts.append(gg * lax.rsqrt(jnp.mean(gg * gg, axis=-1, keepdims=True) + EPS))
    yn = (jnp.concatenate(parts, axis=-1) * ng_ref[...]).astype(BF16)
    o_ref[...] = x_ref[...] + _dot(yn, wo_ref[...])


def _ssd_layer(x2, batch, seq, norm_g, in_w, conv_w, conv_b, dt_bias, a_log, d_skip, ssd_norm_g, out_w):
    m = x2.shape[0]
    tm = ROW_TILE
    gn = SSD_GROUPS * SSD_STATE
    wz = in_w[:, :SSD_D_INNER].astype(BF16)
    wxbc = in_w[:, SSD_D_INNER:SSD_D_INNER + SSD_CONV_CH].astype(BF16)
    pad = V7X_LANES - SSD_HEADS
    wdt = jnp.pad(in_w[:, SSD_D_INNER + SSD_CONV_CH:], ((0, 0), (0, pad))).astype(BF16)
    dtb = jnp.pad(dt_bias, (0, pad))[None, :]
    alog = jnp.pad(a_log, (0, pad))[None, :]

    def rows(width):
        return pl.BlockSpec((tm, width), lambda i: (i, 0))

    z, xs, bm, cm, dt = pl.pallas_call(
        functools.partial(_ssd_in_kernel, tiles_per_seq=seq // tm),
        grid=(m // tm,),
        in_specs=[rows(D_MODEL), _const_spec((1, D_MODEL)), _const_spec(wz.shape), _const_spec(wxbc.shape),
                  _const_spec(wdt.shape), _const_spec(conv_w.shape), _const_spec((1, SSD_CONV_CH)),
                  _const_spec((1, V7X_LANES))],
        out_specs=[rows(SSD_D_INNER), rows(SSD_D_INNER), rows(gn), rows(gn), rows(V7X_LANES)],
        out_shape=[jax.ShapeDtypeStruct((m, SSD_D_INNER), BF16), jax.ShapeDtypeStruct((m, SSD_D_INNER), BF16),
                   jax.ShapeDtypeStruct((m, gn), BF16), jax.ShapeDtypeStruct((m, gn), BF16),
                   jax.ShapeDtypeStruct((m, V7X_LANES), F32)],
        scratch_shapes=[pltpu.VMEM((SSD_CONV_CH // V7X_LANES, V7X_SUBLANES + tm, V7X_LANES), F32)],
        compiler_params=_params(("arbitrary",)), name="ssd_in",
    )(x2, norm_g[None, :], wz, wxbc, wdt, conv_w, conv_b[None, :], dtb)

    expand = (jnp.arange(V7X_LANES)[:, None] == (jnp.arange(SSD_D_INNER) // SSD_HEAD_DIM)[None, :]).astype(BF16)
    dskip = jnp.repeat(d_skip, SSD_HEAD_DIM)[None, :]
    ts = ROW_TILE
    spt = seq // ts

    def tile(width):
        return pl.BlockSpec((ts, width), lambda b, s: (b * spt + s, 0))

    return pl.pallas_call(
        _ssd_core_kernel,
        grid=(batch, spt),
        in_specs=[tile(SSD_D_INNER), tile(gn), tile(gn), tile(V7X_LANES), tile(SSD_D_INNER), tile(D_MODEL),
                  _const_spec((1, V7X_LANES)), _const_spec((1, SSD_D_INNER)), _const_spec(expand.shape),
                  _const_spec((1, SSD_D_INNER)), _const_spec((SSD_D_INNER, D_MODEL))],
        out_specs=tile(D_MODEL),
        out_shape=jax.ShapeDtypeStruct((m, D_MODEL), F32),
        scratch_shapes=[pltpu.VMEM((SSD_GROUPS, SSD_STATE, SSD_HEADS_PER_GROUP * SSD_HEAD_DIM), F32),
                        pltpu.VMEM((ts, SSD_D_INNER), F32)],
        compiler_params=_params(("arbitrary", "arbitrary")), name="ssd_core",
    )(xs, bm, cm, dt, z, x2, alog, dskip, expand, ssd_norm_g[None, :], out_w.astype(BF16))


def _attn_in_kernel(x_ref, g_ref, wqk_ref, wvt_ref, qk_ref, vt_ref):
    t = ATTN_TILE
    h = _rmsnorm(x_ref[...], g_ref[...]).astype(BF16)
    for c0 in range(0, wqk_ref.shape[1], D_MODEL):
        cols = slice(c0, c0 + D_MODEL)
        qk_ref[:, cols] = _dot(h, wqk_ref[:, cols]).astype(BF16)
    tm = x_ref.shape[0]
    vt = _dot_nt(wvt_ref[...], h).astype(BF16)
    ones = jnp.ones((ATTN_V_ROWS - ATTN_V_DIM, tm), BF16)
    pieces = []
    for g in range(ATTN_HEADS):
        pieces += [vt[g * ATTN_V_DIM:(g + 1) * ATTN_V_DIM], ones]
    vt = jnp.concatenate(pieces, axis=0)
    for j in range(tm // t):
        vt_ref[j] = vt[:, j * t:(j + 1) * t]


def _matmul_residual_kernel(x_ref, a_ref, w_ref, o_ref):
    o_ref[...] = x_ref[...] + _dot(a_ref[...], w_ref[...])


def _flash_kernel(q_ref, k_ref, vt_ref, c_ref, s1_ref, s2_ref, qg_ref, kg_ref, seg_ref, perm_ref,
                  lq1_ref, lk1_ref, lq2_ref, lk2_ref, sg_ref, o_ref, kh_ref, acc_ref, *, lambda_init):
    t = ATTN_TILE
    hw = 2 * ATTN_HEAD_DIM
    seq = k_ref.shape[0]
    qi = pl.program_id(2)
    left_row = lax.broadcasted_iota(jnp.int32, (1, V7X_LANES), 1) < ATTN_HEAD_DIM
    heads = range(ATTN_HEADS_PER_STEP)

    def norm_rope(x, r0, nrows, g):
        rows = pl.ds(r0, nrows)
        ss = _dot((x * x).astype(BF16), seg_ref[...])
        xn = x * lax.rsqrt(ss * (1.0 / ATTN_HEAD_DIM) + EPS) * g
        rot = _dot(xn.astype(BF16), perm_ref[...])
        return xn * c_ref[rows, :] + rot[:, :hw] * s1_ref[rows, :] + rot[:, hw:] * s2_ref[rows, :]

    @pl.when(qi == 0)
    def _():
        def prep(j, carry):
            r0 = pl.multiple_of(j * K_PREP_ROWS, K_PREP_ROWS)
            for g in heads:
                cols = slice(g * hw, (g + 1) * hw)
                kx = k_ref[pl.ds(r0, K_PREP_ROWS), cols].astype(F32)
                kh_ref[pl.ds(r0, K_PREP_ROWS), cols] = norm_rope(kx, r0, K_PREP_ROWS, kg_ref[...]).astype(BF16)
            return carry
        lax.fori_loop(0, seq // K_PREP_ROWS, prep, 0)

    q0row = pl.multiple_of(qi * t, t)
    scale = LOG2_E / math.sqrt(ATTN_HEAD_DIM)
    qa, qb = [], []
    for g in heads:
        qh = norm_rope(q_ref[:, g * hw:(g + 1) * hw].astype(F32), q0row, t, qg_ref[...]) * scale
        qa.append(jnp.where(left_row, qh, 0.0).astype(BF16))
        qb.append(jnp.where(left_row, 0.0, qh).astype(BF16))

    def kv_tile(kt, carry, diagonal):
        rows = pl.ds(pl.multiple_of(kt * t, t), t)
        scores = []
        for g in heads:
            ks = kh_ref[rows, g * hw:(g + 1) * hw]
            scores.append((_dot_nt(ks, qa[g]), _dot_nt(ks, qb[g])))
        if diagonal:
            kc = lax.broadcasted_iota(jnp.int32, (t, t), 0) // CHUNK
            qc = lax.broadcasted_iota(jnp.int32, (t, t), 1) // CHUNK
            allowed = kc <= qc
            scores = [tuple(jnp.where(allowed, s, -jnp.inf) for s in pair) for pair in scores]
        probs, out = [], []
        for g in heads:
            new, alphas, ps = [], [], []
            for c, s in enumerate(scores[g]):
                m = jnp.max(s, axis=0, keepdims=True)
                if carry is not None:
                    m_old = carry[g][c]
                    m = jnp.maximum(m_old, m)
                    alphas.append(jnp.exp2(m_old - m))
                ps.append(jnp.exp2(s - m))
                new.append(m)
            probs.append((jnp.concatenate(ps, axis=1).astype(BF16), alphas))
            out.append(tuple(new))
        for g in heads:
            p, alphas = probs[g]
            pv = _dot(vt_ref[kt, g * ATTN_V_ROWS:(g + 1) * ATTN_V_ROWS, :], p)
            if carry is None:
                acc_ref[g] = pv
            else:
                acc_ref[g] = acc_ref[g] * jnp.concatenate(alphas, axis=1) + pv
        return tuple(out)

    stats = kv_tile(qi, None, True)
    stats = lax.fori_loop(0, qi, functools.partial(kv_tile, diagonal=False), stats)

    lam = (jnp.exp(jnp.sum(lq1_ref[...] * lk1_ref[...], axis=-1, keepdims=True))
           - jnp.exp(jnp.sum(lq2_ref[...] * lk2_ref[...], axis=-1, keepdims=True)) + lambda_init)
    for g in heads:
        acc = acc_ref[g]
        la = acc[ATTN_V_DIM:ATTN_V_DIM + 1, :t]
        lb = acc[ATTN_V_DIM:ATTN_V_DIM + 1, t:]
        ot = acc[:ATTN_V_DIM, :t] / la - lam * (acc[:ATTN_V_DIM, t:] / lb)
        ot = ot * lax.rsqrt(jnp.mean(ot * ot, axis=0, keepdims=True) + EPS)
        o_ref[:, g * hw:(g + 1) * hw] = (ot.T * sg_ref[...] * (1.0 - lambda_init)).astype(o_ref.dtype)


def _attn_layer(x2, batch, seq, tables, norm_g, in_w, q_norm_g, k_norm_g, lq1, lk1, lq2, lk2,
                subln_g, out_w, lambda_init):
    m = x2.shape[0]
    tm = ROW_TILE
    t = ATTN_TILE
    qt = seq // t
    wqk = in_w[:, :2 * D_MODEL].astype(BF16)
    wvt = in_w[:, 2 * D_MODEL:].T.astype(BF16)
    qk, vt = pl.pallas_call(
        _attn_in_kernel,
        grid=(m // tm,),
        in_specs=[pl.BlockSpec((tm, D_MODEL), lambda i: (i, 0)), _const_spec((1, D_MODEL)),
                  _const_spec(wqk.shape), _const_spec(wvt.shape)],
        out_specs=[pl.BlockSpec((tm, 2 * D_MODEL), lambda i: (i, 0)),
                   pl.BlockSpec((tm // t, ATTN_HEADS * ATTN_V_ROWS, t), lambda i: (i, 0, 0))],
        out_shape=[jax.ShapeDtypeStruct((m, 2 * D_MODEL), BF16),
                   jax.ShapeDtypeStruct((m // t, ATTN_HEADS * ATTN_V_ROWS, t), BF16)],
        compiler_params=_params(("arbitrary",)), name="attn_in",
    )(x2, norm_g[None, :], wqk, wvt)

    hw = 2 * ATTN_HEAD_DIM
    gw = ATTN_HEADS_PER_STEP * hw
    ng = ATTN_HEADS // ATTN_HEADS_PER_STEP
    cos_t, sin1_t, sin2_t = tables
    seq_rows = pl.BlockSpec((seq, V7X_LANES), lambda b, h, q: (b, 0))
    vec = pl.BlockSpec((1, hw), lambda b, h, q: (0, 0))
    vec64 = pl.BlockSpec((1, ATTN_HEAD_DIM), lambda b, h, q: (0, 0))
    lane = jnp.arange(hw)
    seg = (lane[:, None] // ATTN_HEAD_DIM == lane[None, :] // ATTN_HEAD_DIM).astype(BF16)
    half = ROPE_DIM // 2
    perm = jnp.concatenate([lane[:, None] == lane[None, :] + half, lane[:, None] == lane[None, :] - half],
                           axis=1).astype(BF16)
    o = pl.pallas_call(
        functools.partial(_flash_kernel, lambda_init=lambda_init),
        grid=(batch, ng, qt),
        in_specs=[pl.BlockSpec((t, gw), lambda b, h, q: (b * qt + q, h)),
                  pl.BlockSpec((seq, gw), lambda b, h, q: (b, ng + h)),
                  pl.BlockSpec((qt, ATTN_HEADS_PER_STEP * ATTN_V_ROWS, t), lambda b, h, q: (b, h, 0)),
                  seq_rows, seq_rows, seq_rows, vec, vec,
                  pl.BlockSpec((hw, hw), lambda b, h, q: (0, 0)), pl.BlockSpec((hw, 2 * hw), lambda b, h, q: (0, 0)),
                  vec64, vec64, vec64, vec64, vec],
        out_specs=pl.BlockSpec((t, gw), lambda b, h, q: (b * qt + q, h)),
        out_shape=jax.ShapeDtypeStruct((m, D_MODEL), BF16),
        scratch_shapes=[pltpu.VMEM((seq, gw), BF16),
                        pltpu.VMEM((ATTN_HEADS_PER_STEP, ATTN_V_ROWS, 2 * t), F32)],
        compiler_params=_params(("arbitrary", "arbitrary", "arbitrary")), name="flash",
    )(qk, qk, vt, cos_t, sin1_t, sin2_t, jnp.tile(q_norm_g, 2)[None, :], jnp.tile(k_norm_g, 2)[None, :],
      seg, perm, lq1[None, :], lk1[None, :], lq2[None, :], lk2[None, :], subln_g[None, :])

    return pl.pallas_call(
        _matmul_residual_kernel,
        grid=(m // tm,),
        in_specs=[pl.BlockSpec((tm, D_MODEL), lambda i: (i, 0)), pl.BlockSpec((tm, D_MODEL), lambda i: (i, 0)),
                  _const_spec((D_MODEL, D_MODEL))],
        out_specs=pl.BlockSpec((tm, D_MODEL), lambda i: (i, 0)),
        out_shape=jax.ShapeDtypeStruct((m, D_MODEL), F32),
        compiler_params=_params(("arbitrary",)), name="attn_out",
    )(x2, o, out_w.astype(BF16))


def _ffn_kernel(x_ref, g_ref, wg_ref, wu_ref, cwg_ref, cwu_ref, cbg_ref, cbu_ref, wd_ref,
                o_ref, buf_g_ref, buf_u_ref, act_ref, *, tiles_per_seq):
    tm = x_ref.shape[0]
    first = pl.program_id(0) % tiles_per_seq == 0
    _conv_history_reset(buf_g_ref, first)
    _conv_history_reset(buf_u_ref, first)
    x = x_ref[...]
    h = _rmsnorm(x, g_ref[...]).astype(BF16)
    for c0 in range(0, D_FF, FFN_COLS):
        _conv_store(buf_g_ref, c0, _dot(h, wg_ref[:, c0:c0 + FFN_COLS]))
        _conv_store(buf_u_ref, c0, _dot(h, wu_ref[:, c0:c0 + FFN_COLS]))
        for l0 in range(c0, c0 + FFN_COLS, V7X_LANES):
            lanes = slice(l0, l0 + V7X_LANES)
            yg = _causal_conv(buf_g_ref, l0 // V7X_LANES, cwg_ref[:, lanes], cbg_ref[:, lanes])
            yu = _causal_conv(buf_u_ref, l0 // V7X_LANES, cwu_ref[:, lanes], cbu_ref[:, lanes])
            act_ref[:, lanes] = (_silu(yg) * yu).astype(BF16)
    _conv_history_carry(buf_g_ref)
    _conv_history_carry(buf_u_ref)
    o_ref[...] = x + _dot(act_ref[...], wd_ref[...])


def _ffn_layer(x2, seq, norm_g, up_w, conv_w, conv_b, down_w):
    m = x2.shape[0]
    tm = ROW_TILE
    wg = up_w[:, :D_FF].astype(BF16)
    wu = up_w[:, D_FF:].astype(BF16)
    row = pl.BlockSpec((tm, D_MODEL), lambda i: (i, 0))
    return pl.pallas_call(
        functools.partial(_ffn_kernel, tiles_per_seq=seq // tm),
        grid=(m // tm,),
        in_specs=[row, _const_spec((1, D_MODEL)), _const_spec((D_MODEL, D_FF)), _const_spec((D_MODEL, D_FF)),
                  _const_spec((FFN_CONV, D_FF)), _const_spec((FFN_CONV, D_FF)), _const_spec((1, D_FF)),
                  _const_spec((1, D_FF)), _const_spec((D_FF, D_MODEL))],
        out_specs=row,
        out_shape=jax.ShapeDtypeStruct((m, D_MODEL), F32),
        scratch_shapes=[pltpu.VMEM((D_FF // V7X_LANES, V7X_SUBLANES + tm, V7X_LANES), F32),
                        pltpu.VMEM((D_FF // V7X_LANES, V7X_SUBLANES + tm, V7X_LANES), F32),
                        pltpu.VMEM((tm, D_FF), BF16)],
        compiler_params=_params(("arbitrary",)), name="ffn",
    )(x2, norm_g[None, :], wg, wu, conv_w[:, :D_FF], conv_w[:, D_FF:], conv_b[None, :D_FF],
      conv_b[None, D_FF:], down_w.astype(BF16))


def kernel(x, positions, norm_mix_g, norm_ffn_g, ssd_in_w, ssd_conv_w, ssd_conv_b, ssd_dt_bias, ssd_a_log, ssd_d, ssd_norm_g, ssd_out_w, attn_in_w, attn_q_norm_g, attn_k_norm_g, attn_lq1, attn_lk1, attn_lq2, attn_lk2, attn_subln_g, attn_out_w, ffn_up_w, ffn_conv_w, ffn_conv_b, ffn_down_w):
    batch, seq, d = x.shape
    depth = norm_mix_g.shape[0]
    assert d == D_MODEL and seq % ROW_TILE == 0 and seq % ATTN_TILE == 0
    x2 = x.reshape(batch * seq, d)
    tables = _rope_tables(positions)
    for i in range(depth):
        j = i // N_MIXERS
        if i % N_MIXERS == 0:
            x2 = _ssd_layer(x2, batch, seq, norm_mix_g[i], ssd_in_w[j], ssd_conv_w[j], ssd_conv_b[j],
                            ssd_dt_bias[j], ssd_a_log[j], ssd_d[j], ssd_norm_g[j], ssd_out_w[j])
        else:
            lambda_init = 0.8 - 0.6 * math.exp(-0.3 * i)
            x2 = _attn_layer(x2, batch, seq, tables, norm_mix_g[i], attn_in_w[j], attn_q_norm_g[j],
                             attn_k_norm_g[j], attn_lq1[j], attn_lk1[j], attn_lq2[j], attn_lk2[j],
                             attn_subln_g[j], attn_out_w[j], lambda_init)
        x2 = _ffn_layer(x2, seq, norm_ffn_g[i], ffn_up_w[i], ffn_conv_w[i], ffn_conv_b[i], ffn_down_w[i])
    return x2.reshape(batch, seq, d)
```

```python
import functools
import math

import jax
import jax.numpy as jnp
from jax import lax
from jax.experimental import pallas as pl
from jax.experimental.pallas import tpu as pltpu

F32 = jnp.float32
BF16 = jnp.bfloat16

V7X_LANES = 128
V7X_SUBLANES = 8
V7X_BF16_SUBLANES = 16
V7X_VMEM_LIMIT_BYTES = 56 * 1024 * 1024

D_MODEL = 1024
CHUNK = 64
EPS = 1e-6
SSD_D_INNER = 2048
SSD_HEAD_DIM = 64
SSD_HEADS = 32
SSD_GROUPS = 4
SSD_HEADS_PER_GROUP = 8
SSD_STATE = 128
SSD_CONV = 4
SSD_CONV_CH = SSD_D_INNER + 2 * SSD_GROUPS * SSD_STATE
ATTN_HEADS = 8
ATTN_HEAD_DIM = 64
ATTN_V_DIM = 128
ATTN_V_ROWS = ATTN_V_DIM + V7X_BF16_SUBLANES
LOG2_E = math.log2(math.e)
ROPE_THETA = 500000.0
ROPE_DIM = 16
D_FF = 2816
FFN_CONV = 3
N_MIXERS = 2

ROW_TILE = 512
SSD_SCAN_CHUNK = 128
SSD_CONV_COLS = 512
SSD_OUT_ROWS = 256
FFN_COLS = 256
ATTN_QUERY_TILE = 256
ATTN_KEY_TILE = 256


def _dot(a, b):
    return jnp.dot(a, b, preferred_element_type=F32)


def _dot_nt(a, b):
    return lax.dot_general(a, b, (((1,), (1,)), ((), ())), preferred_element_type=F32)


def _dot_tn(a, b):
    return lax.dot_general(a, b, (((0,), (0,)), ((), ())), preferred_element_type=F32)


def _silu(x):
    h = 0.5 * x
    return h + h * jnp.tanh(h)


def _rmsnorm(x, g):
    return x * lax.rsqrt(jnp.mean(x * x, axis=-1, keepdims=True) + EPS) * g


def _conv_store(buf_ref, c0, a):
    for j in range(a.shape[1] // V7X_LANES):
        buf_ref[c0 // V7X_LANES + j, V7X_SUBLANES:, :] = a[:, j * V7X_LANES:(j + 1) * V7X_LANES]


def _causal_conv(buf_ref, slab, w, b):
    kw = w.shape[0]
    tm = buf_ref.shape[1] - V7X_SUBLANES
    y = b + w[kw - 1:kw] * buf_ref[slab, V7X_SUBLANES:, :]
    for j in range(1, kw):
        y = y + w[kw - 1 - j:kw - j] * buf_ref[slab, V7X_SUBLANES - j:V7X_SUBLANES - j + tm, :]
    return y


def _conv_history_reset(buf_ref, first):
    @pl.when(first)
    def _():
        buf_ref[:, :V7X_SUBLANES, :] = jnp.zeros((buf_ref.shape[0], V7X_SUBLANES, V7X_LANES), buf_ref.dtype)


def _conv_history_carry(buf_ref):
    tm = buf_ref.shape[1] - V7X_SUBLANES
    buf_ref[:, :V7X_SUBLANES, :] = buf_ref[:, tm:, :]


def _const_spec(shape):
    nd = len(shape)
    return pl.BlockSpec(shape, lambda *_: (0,) * nd, pipeline_mode=pl.Buffered(1))


def _params(semantics):
    return pltpu.CompilerParams(dimension_semantics=semantics,
                                vmem_limit_bytes=V7X_VMEM_LIMIT_BYTES)


def _rope_kernel(pos_ref, freq_ref, m1_ref, m2_ref, c_ref, s1_ref, s2_ref):
    ang = pos_ref[...].astype(F32) * freq_ref[...]
    s = jnp.sin(ang)
    c_ref[...] = jnp.cos(ang)
    s1_ref[...] = s * m1_ref[...]
    s2_ref[...] = s * m2_ref[...]


def _rope_tables(positions):
    b, s = positions.shape
    m = b * s
    inv_freq = ROPE_THETA ** (-jnp.arange(0, ROPE_DIM, 2, dtype=F32) / ROPE_DIM)
    d = jnp.arange(V7X_LANES) % ATTN_HEAD_DIM
    half = ROPE_DIM // 2
    freq = jnp.where(d < ROPE_DIM, inv_freq[d % half], 0.0).astype(F32)[None, :]
    m1 = jnp.where(d < half, -1.0, 0.0).astype(F32)[None, :]
    m2 = jnp.where((d >= half) & (d < ROPE_DIM), 1.0, 0.0).astype(F32)[None, :]
    pos = jnp.broadcast_to(positions.reshape(m, 1), (m, V7X_LANES))
    row = pl.BlockSpec((s, V7X_LANES), lambda i: (i, 0))
    vec = pl.BlockSpec((1, V7X_LANES), lambda i: (0, 0))
    out = jax.ShapeDtypeStruct((m, V7X_LANES), F32)
    return pl.pallas_call(
        _rope_kernel, grid=(b,), in_specs=[row, vec, vec, vec], out_specs=[row, row, row],
        out_shape=[out, out, out], compiler_params=_params(("arbitrary",)), name="rope_tables",
    )(pos, freq, m1, m2)


def _ssd_in_kernel(x_ref, g_ref, wzdt_ref, wxbc_ref, cw_ref, cb_ref, dtb_ref,
                   z_ref, xs_ref, b_ref, c_ref, dt_ref, buf_ref, *, tiles_per_seq):
    _conv_history_reset(buf_ref, pl.program_id(0) % tiles_per_seq == 0)
    h = _rmsnorm(x_ref[...], g_ref[...]).astype(BF16)
    zdt = _dot(h, wzdt_ref[...])
    z_ref[...] = zdt[:, :SSD_D_INNER].astype(BF16)
    dt = zdt[:, SSD_D_INNER:] + dtb_ref[...]
    dt_ref[...] = jnp.maximum(dt, 0.0) + jnp.log1p(jnp.exp(-jnp.abs(dt)))
    gn = SSD_GROUPS * SSD_STATE
    for c0 in range(0, SSD_CONV_CH, SSD_CONV_COLS):
        _conv_store(buf_ref, c0, _dot(h, wxbc_ref[:, c0:c0 + SSD_CONV_COLS]))
        for l0 in range(c0, c0 + SSD_CONV_COLS, V7X_LANES):
            lanes = slice(l0, l0 + V7X_LANES)
            y = _silu(_causal_conv(buf_ref, l0 // V7X_LANES, cw_ref[:, lanes], cb_ref[:, lanes])).astype(BF16)
            if l0 < SSD_D_INNER:
                xs_ref[:, lanes] = y
            elif l0 < SSD_D_INNER + gn:
                b_ref[:, l0 - SSD_D_INNER:l0 - SSD_D_INNER + V7X_LANES] = y
            else:
                c_ref[:, l0 - SSD_D_INNER - gn:l0 - SSD_D_INNER - gn + V7X_LANES] = y
    _conv_history_carry(buf_ref)


def _ssd_core_kernel(xs_ref, b_ref, c_ref, dt_ref, z_ref, x_ref, alog_ref, dskip_ref, e_ref,
                     ng_ref, wo_ref, o_ref, h_ref, y_ref):
    ts = xs_ref.shape[0]
    L = SSD_SCAN_CHUNK
    P2 = 2 * SSD_HEAD_DIM
    GP = SSD_HEADS_PER_GROUP * SSD_HEAD_DIM

    @pl.when(pl.program_id(1) == 0)
    def _():
        h_ref[...] = jnp.zeros_like(h_ref)

    row = lax.broadcasted_iota(jnp.int32, (L, L), 0)
    col = lax.broadcasted_iota(jnp.int32, (L, L), 1)
    causal = row >= col
    tril = jnp.where(causal, 1.0, 0.0).astype(BF16)
    left = lax.broadcasted_iota(jnp.int32, (L, P2), 1) < SSD_HEAD_DIM
    a_row = -jnp.exp(alog_ref[...])

    def chunk_body(ci, carry):
        rows = pl.ds(pl.multiple_of(ci * L, L), L)
        dt = dt_ref[rows, :]
        da = dt * a_row
        hi = da.astype(BF16)
        r1 = da - hi.astype(F32)
        mid = r1.astype(BF16)
        lo = (r1 - mid.astype(F32)).astype(BF16)
        cs = _dot(tril, hi) + _dot(tril, mid) + _dot(tril, lo)
        cs_t = cs.T
        ecs = jnp.exp(cs)
        dec = jnp.exp(cs[L - 1:L, :] - cs)
        stacked = jnp.concatenate([dt, dec, ecs], axis=0).astype(BF16)
        wide = _dot(stacked, e_ref[...])
        dt_e, dec_e, ecs_e = wide[:L], wide[L:2 * L], wide[2 * L:]
        xs = xs_ref[rows, :].astype(F32)
        xdt = xs * dt_e
        xdt_b = xdt.astype(BF16)
        dx_b = (xdt * dec_e).astype(BF16)
        for g in range(SSD_GROUPS):
            gc = slice(g * SSD_STATE, (g + 1) * SSD_STATE)
            hc = slice(g * GP, (g + 1) * GP)
            bg = b_ref[rows, gc]
            cg = c_ref[rows, gc]
            cb = _dot_nt(cg, bg)
            hst = h_ref[g]
            y_off = _dot(cg, hst.astype(BF16)) * ecs_e[:, hc]
            h_ref[g] = hst * ecs_e[L - 1:L, hc] + _dot_tn(bg, dx_b[:, hc])
            for p in range(SSD_HEADS_PER_GROUP // 2):
                h0 = g * SSD_HEADS_PER_GROUP + 2 * p
                pc = slice(h0 * SSD_HEAD_DIM, h0 * SSD_HEAD_DIM + P2)
                pair = xdt_b[:, pc]

                def lmat(hd):
                    seg = cs[:, hd:hd + 1] - cs_t[hd:hd + 1, :]
                    return (cb * jnp.exp(jnp.where(causal, seg, -jnp.inf))).astype(BF16)

                lhs = jnp.concatenate([lmat(h0), lmat(h0 + 1)], axis=1)
                rhs = jnp.concatenate([jnp.where(left, pair, 0), jnp.where(left, 0, pair)], axis=0)
                y_ref[rows, pc] = (_dot(lhs, rhs) + y_off[:, p * P2:(p + 1) * P2]
                                   + xs[:, pc] * dskip_ref[:, pc])
        return carry

    lax.fori_loop(0, ts // L, chunk_body, 0)

    gw = SSD_D_INNER // SSD_GROUPS
    for r0 in range(0, ts, SSD_OUT_ROWS):
        rows = slice(r0, r0 + SSD_OUT_ROWS)
        gated = y_ref[rows, :] * _silu(z_ref[rows, :].astype(F32))
        parts = []
        for g in range(SSD_GROUPS):
            gg = gated[:, g * gw:(g + 1) * gw]
            parts.append(gg * lax.rsqrt(jnp.mean(gg * gg, axis=-1, keepdims=True) + EPS))
        yn = (jnp.concatenate(parts, axis=-1) * ng_ref[...]).astype(BF16)
        o_ref[rows, :] = x_ref[rows, :] + _dot(yn, wo_ref[...])


def _ssd_layer(x2, batch, seq, norm_g, in_w, conv_w, conv_b, dt_bias, a_log, d_skip, ssd_norm_g, out_w):
    m = x2.shape[0]
    tm = ROW_TILE
    gn = SSD_GROUPS * SSD_STATE
    wxbc = in_w[:, SSD_D_INNER:SSD_D_INNER + SSD_CONV_CH].astype(BF16)
    pad = V7X_LANES - SSD_HEADS
    wzdt = jnp.pad(jnp.concatenate([in_w[:, :SSD_D_INNER], in_w[:, SSD_D_INNER + SSD_CONV_CH:]], axis=1),
                   ((0, 0), (0, pad))).astype(BF16)
    dtb = jnp.pad(dt_bias, (0, pad))[None, :]
    alog = jnp.pad(a_log, (0, pad))[None, :]

    def rows(width):
        return pl.BlockSpec((tm, width), lambda i: (i, 0))

    z, xs, bm, cm, dt = pl.pallas_call(
        functools.partial(_ssd_in_kernel, tiles_per_seq=seq // tm),
        grid=(m // tm,),
        in_specs=[rows(D_MODEL), _const_spec((1, D_MODEL)), _const_spec(wzdt.shape), _const_spec(wxbc.shape),
                  _const_spec(conv_w.shape), _const_spec((1, SSD_CONV_CH)), _const_spec((1, V7X_LANES))],
        out_specs=[rows(SSD_D_INNER), rows(SSD_D_INNER), rows(gn), rows(gn), rows(V7X_LANES)],
        out_shape=[jax.ShapeDtypeStruct((m, SSD_D_INNER), BF16), jax.ShapeDtypeStruct((m, SSD_D_INNER), BF16),
                   jax.ShapeDtypeStruct((m, gn), BF16), jax.ShapeDtypeStruct((m, gn), BF16),
                   jax.ShapeDtypeStruct((m, V7X_LANES), F32)],
        scratch_shapes=[pltpu.VMEM((SSD_CONV_CH // V7X_LANES, V7X_SUBLANES + tm, V7X_LANES), F32)],
        compiler_params=_params(("arbitrary",)), name="ssd_in",
    )(x2, norm_g[None, :], wzdt, wxbc, conv_w, conv_b[None, :], dtb)

    expand = (jnp.arange(V7X_LANES)[:, None] == (jnp.arange(SSD_D_INNER) // SSD_HEAD_DIM)[None, :]).astype(BF16)
    dskip = jnp.repeat(d_skip, SSD_HEAD_DIM)[None, :]
    ts = ROW_TILE
    spt = seq // ts

    def tile(width):
        return pl.BlockSpec((ts, width), lambda b, s: (b * spt + s, 0))

    return pl.pallas_call(
        _ssd_core_kernel,
        grid=(batch, spt),
        in_specs=[tile(SSD_D_INNER), tile(gn), tile(gn), tile(V7X_LANES), tile(SSD_D_INNER), tile(D_MODEL),
                  _const_spec((1, V7X_LANES)), _const_spec((1, SSD_D_INNER)), _const_spec(expand.shape),
                  _const_spec((1, SSD_D_INNER)), _const_spec((SSD_D_INNER, D_MODEL))],
        out_specs=tile(D_MODEL),
        out_shape=jax.ShapeDtypeStruct((m, D_MODEL), F32),
        scratch_shapes=[pltpu.VMEM((SSD_GROUPS, SSD_STATE, SSD_HEADS_PER_GROUP * SSD_HEAD_DIM), F32),
                        pltpu.VMEM((ts, SSD_D_INNER), F32)],
        compiler_params=_params(("arbitrary", "arbitrary")), name="ssd_core",
    )(xs, bm, cm, dt, z, x2, alog, dskip, expand, ssd_norm_g[None, :], out_w.astype(BF16))


def _attn_in_kernel(x_ref, g_ref, wqk_ref, wvt_ref, c_ref, s1_ref, s2_ref, qg_ref, kg_ref, seg_ref,
                    q_ref, k_ref, vt_ref):
    t = ATTN_KEY_TILE
    hw = 2 * ATTN_HEAD_DIM
    half = ROPE_DIM // 2
    tm = x_ref.shape[0]
    h = _rmsnorm(x_ref[...], g_ref[...]).astype(BF16)
    cos_t, sin1_t, sin2_t = c_ref[...], s1_ref[...], s2_ref[...]
    q_scale = LOG2_E / math.sqrt(ATTN_HEAD_DIM)
    for c0, o_ref, gain_ref, scale in ((0, q_ref, qg_ref, q_scale), (D_MODEL, k_ref, kg_ref, None)):
        a = _dot(h, wqk_ref[:, c0:c0 + D_MODEL])
        for p0 in range(0, D_MODEL, 2 * hw):
            pair = a[:, p0:p0 + 2 * hw]
            ss = _dot((pair * pair).astype(BF16), seg_ref[...])
            inv = lax.rsqrt(ss * (1.0 / ATTN_HEAD_DIM) + EPS)
            for j in range(2):
                lanes = slice(j * hw, (j + 1) * hw)
                xn = pair[:, lanes] * inv[:, lanes] * gain_ref[...]
                y = (xn * cos_t + pltpu.roll(xn, hw - half, axis=1) * sin1_t
                     + pltpu.roll(xn, half, axis=1) * sin2_t)
                o_ref[:, p0 + j * hw:p0 + (j + 1) * hw] = (y if scale is None else y * scale).astype(BF16)
    vt = _dot_nt(wvt_ref[...], h).astype(BF16)
    ones = jnp.ones((ATTN_V_ROWS - ATTN_V_DIM, tm), BF16)
    pieces = []
    for g in range(ATTN_HEADS):
        pieces += [vt[g * ATTN_V_DIM:(g + 1) * ATTN_V_DIM], ones]
    vt = jnp.concatenate(pieces, axis=0)
    for j in range(tm // t):
        vt_ref[j] = vt[:, j * t:(j + 1) * t]


def _flash_kernel(q_ref, k_ref, vt_ref, x_ref, lq1_ref, lk1_ref, lq2_ref, lk2_ref, sg_ref, wo_ref,
                  o_ref, acc_ref, *, lambda_init):
    tq = q_ref.shape[0]
    tk = ATTN_KEY_TILE
    hw = 2 * ATTN_HEAD_DIM
    qi = pl.program_id(1)
    left_row = lax.broadcasted_iota(jnp.int32, (1, V7X_LANES), 1) < ATTN_HEAD_DIM
    heads = range(ATTN_HEADS)
    kt0 = qi * (tq // tk)

    qa, qb = [], []
    for g in heads:
        qh = q_ref[:, g * hw:(g + 1) * hw]
        qa.append(jnp.where(left_row, qh, 0))
        qb.append(jnp.where(left_row, 0, qh))

    def kv_tile(kt, carry, diagonal=None):
        rows = pl.ds(pl.multiple_of(kt * tk, tk), tk)
        scores = []
        for g in heads:
            ks = k_ref[rows, g * hw:(g + 1) * hw]
            scores.append((_dot_nt(ks, qa[g]), _dot_nt(ks, qb[g])))
        if diagonal is not None:
            kc = lax.broadcasted_iota(jnp.int32, (tk, tq), 0) // CHUNK + diagonal * (tk // CHUNK)
            qc = lax.broadcasted_iota(jnp.int32, (tk, tq), 1) // CHUNK
            allowed = kc <= qc
            scores = [tuple(jnp.where(allowed, s, -jnp.inf) for s in pair) for pair in scores]
        probs, out = [], []
        for g in heads:
            new, alphas, ps = [], [], []
            for c, s in enumerate(scores[g]):
                m = jnp.max(s, axis=0, keepdims=True)
                if carry is not None:
                    m_old = carry[g][c]
                    m = jnp.maximum(m_old, m)
                    alphas.append(jnp.exp2(m_old - m))
                ps.append(jnp.exp2(s - m))
                new.append(m)
            probs.append((jnp.concatenate(ps, axis=1).astype(BF16), alphas))
            out.append(tuple(new))
        for g in heads:
            p, alphas = probs[g]
            pv = _dot(vt_ref[kt, g * ATTN_V_ROWS:(g + 1) * ATTN_V_ROWS, :], p)
            if carry is None:
                acc_ref[g] = pv
            else:
                acc_ref[g] = acc_ref[g] * jnp.concatenate(alphas, axis=1) + pv
        return tuple(out)

    stats = kv_tile(kt0, None, 0)
    for d in range(1, tq // tk):
        stats = kv_tile(kt0 + d, stats, d)
    lax.fori_loop(0, kt0, kv_tile, stats)

    lam = (jnp.exp(jnp.sum(lq1_ref[...] * lk1_ref[...], axis=-1, keepdims=True))
           - jnp.exp(jnp.sum(lq2_ref[...] * lk2_ref[...], axis=-1, keepdims=True)) + lambda_init)
    outs = []
    for g in heads:
        acc = acc_ref[g]
        la = acc[ATTN_V_DIM:ATTN_V_DIM + 1, :tq]
        lb = acc[ATTN_V_DIM:ATTN_V_DIM + 1, tq:]
        ot = acc[:ATTN_V_DIM, :tq] / la - lam * (acc[:ATTN_V_DIM, tq:] / lb)
        ot = ot * lax.rsqrt(jnp.mean(ot * ot, axis=0, keepdims=True) + EPS)
        outs.append((ot.T * sg_ref[...] * (1.0 - lambda_init)).astype(BF16))
    o_ref[...] = x_ref[...] + _dot(jnp.concatenate(outs, axis=1), wo_ref[...])


def _attn_layer(x2, batch, seq, tables, norm_g, in_w, q_norm_g, k_norm_g, lq1, lk1, lq2, lk2,
                subln_g, out_w, lambda_init):
    m = x2.shape[0]
    tm = ROW_TILE
    tk = ATTN_KEY_TILE
    tq = ATTN_QUERY_TILE
    qt = seq // tq
    hw = 2 * ATTN_HEAD_DIM
    wqk = in_w[:, :2 * D_MODEL].astype(BF16)
    wvt = in_w[:, 2 * D_MODEL:].T.astype(BF16)
    cos_t, sin1_t, sin2_t = tables
    lane = jnp.arange(2 * hw)
    seg = (lane[:, None] // ATTN_HEAD_DIM == lane[None, :] // ATTN_HEAD_DIM).astype(BF16)
    rows_spec = pl.BlockSpec((tm, D_MODEL), lambda i: (i, 0))
    table_spec = pl.BlockSpec((tm, V7X_LANES), lambda i: (i, 0))
    qkv_shape = jax.ShapeDtypeStruct((m, D_MODEL), BF16)
    q, k, vt = pl.pallas_call(
        _attn_in_kernel,
        grid=(m // tm,),
        in_specs=[rows_spec, _const_spec((1, D_MODEL)), _const_spec(wqk.shape), _const_spec(wvt.shape),
                  table_spec, table_spec, table_spec, _const_spec((1, hw)), _const_spec((1, hw)),
                  _const_spec(seg.shape)],
        out_specs=[rows_spec, rows_spec,
                   pl.BlockSpec((tm // tk, ATTN_HEADS * ATTN_V_ROWS, tk), lambda i: (i, 0, 0))],
        out_shape=[qkv_shape, qkv_shape,
                   jax.ShapeDtypeStruct((m // tk, ATTN_HEADS * ATTN_V_ROWS, tk), BF16)],
        compiler_params=_params(("arbitrary",)), name="attn_in",
    )(x2, norm_g[None, :], wqk, wvt, cos_t, sin1_t, sin2_t, jnp.tile(q_norm_g, 2)[None, :],
      jnp.tile(k_norm_g, 2)[None, :], seg)

    vec64 = _const_spec((1, ATTN_HEAD_DIM))
    tile_spec = pl.BlockSpec((tq, D_MODEL), lambda b, qi: (b * qt + qi, 0))
    return pl.pallas_call(
        functools.partial(_flash_kernel, lambda_init=lambda_init),
        grid=(batch, qt),
        in_specs=[tile_spec,
                  pl.BlockSpec((seq, D_MODEL), lambda b, qi: (b, 0)),
                  pl.BlockSpec((seq // tk, ATTN_HEADS * ATTN_V_ROWS, tk), lambda b, qi: (b, 0, 0)),
                  tile_spec, vec64, vec64, vec64, vec64, _const_spec((1, ATTN_V_DIM)),
                  _const_spec((D_MODEL, D_MODEL))],
        out_specs=tile_spec,
        out_shape=jax.ShapeDtypeStruct((m, D_MODEL), F32),
        scratch_shapes=[pltpu.VMEM((ATTN_HEADS, ATTN_V_ROWS, 2 * tq), F32)],
        compiler_params=_params(("arbitrary", "arbitrary")), name="flash",
    )(q, k, vt, x2, lq1[None, :], lk1[None, :], lq2[None, :], lk2[None, :], subln_g[None, :],
      out_w.astype(BF16))


def _ffn_kernel(x_ref, g_ref, wg_ref, wu_ref, cwg_ref, cwu_ref, cbg_ref, cbu_ref, wd_ref,
                o_ref, buf_g_ref, buf_u_ref, act_ref, *, tiles_per_seq):
    tm = x_ref.shape[0]
    first = pl.program_id(0) % tiles_per_seq == 0
    _conv_history_reset(buf_g_ref, first)
    _conv_history_reset(buf_u_ref, first)
    x = x_ref[...]
    h = _rmsnorm(x, g_ref[...]).astype(BF16)
    for c0 in range(0, D_FF, FFN_COLS):
        _conv_store(buf_g_ref, c0, _dot(h, wg_ref[:, c0:c0 + FFN_COLS]))
        _conv_store(buf_u_ref, c0, _dot(h, wu_ref[:, c0:c0 + FFN_COLS]))
        for l0 in range(c0, c0 + FFN_COLS, V7X_LANES):
            lanes = slice(l0, l0 + V7X_LANES)
            yg = _causal_conv(buf_g_ref, l0 // V7X_LANES, cwg_ref[:, lanes], cbg_ref[:, lanes])
            yu = _causal_conv(buf_u_ref, l0 // V7X_LANES, cwu_ref[:, lanes], cbu_ref[:, lanes])
            act_ref[:, lanes] = (_silu(yg) * yu).astype(BF16)
    _conv_history_carry(buf_g_ref)
    _conv_history_carry(buf_u_ref)
    o_ref[...] = x + _dot(act_ref[...], wd_ref[...])


def _ffn_layer(x2, seq, norm_g, up_w, conv_w, conv_b, down_w):
    m = x2.shape[0]
    tm = ROW_TILE
    wg = up_w[:, :D_FF].astype(BF16)
    wu = up_w[:, D_FF:].astype(BF16)
    row = pl.BlockSpec((tm, D_MODEL), lambda i: (i, 0))
    return pl.pallas_call(
        functools.partial(_ffn_kernel, tiles_per_seq=seq // tm),
        grid=(m // tm,),
        in_specs=[row, _const_spec((1, D_MODEL)), _const_spec((D_MODEL, D_FF)), _const_spec((D_MODEL, D_FF)),
                  _const_spec((FFN_CONV, D_FF)), _const_spec((FFN_CONV, D_FF)), _const_spec((1, D_FF)),
                  _const_spec((1, D_FF)), _const_spec((D_FF, D_MODEL))],
        out_specs=row,
        out_shape=jax.ShapeDtypeStruct((m, D_MODEL), F32),
        scratch_shapes=[pltpu.VMEM((D_FF // V7X_LANES, V7X_SUBLANES + tm, V7X_LANES), F32),
                        pltpu.VMEM((D_FF // V7X_LANES, V7X_SUBLANES + tm, V7X_LANES), F32),
                        pltpu.VMEM((tm, D_FF), BF16)],
        compiler_params=_params(("arbitrary",)), name="ffn",
    )(x2, norm_g[None, :], wg, wu, conv_w[:, :D_FF], conv_w[:, D_FF:], conv_b[None, :D_FF],
      conv_b[None, D_FF:], down_w.astype(BF16))


def kernel(x, positions, norm_mix_g, norm_ffn_g, ssd_in_w, ssd_conv_w, ssd_conv_b, ssd_dt_bias, ssd_a_log, ssd_d, ssd_norm_g, ssd_out_w, attn_in_w, attn_q_norm_g, attn_k_norm_g, attn_lq1, attn_lk1, attn_lq2, attn_lk2, attn_subln_g, attn_out_w, ffn_up_w, ffn_conv_w, ffn_conv_b, ffn_down_w):
    batch, seq, d = x.shape
    depth = norm_mix_g.shape[0]
    assert d == D_MODEL and seq % ROW_TILE == 0 and seq % ATTN_QUERY_TILE == 0
    assert ROW_TILE % ATTN_KEY_TILE == 0 and ATTN_QUERY_TILE % ATTN_KEY_TILE == 0 and ATTN_KEY_TILE % CHUNK == 0
    x2 = x.reshape(batch * seq, d)
    tables = _rope_tables(positions)
    for i in range(depth):
        j = i // N_MIXERS
        if i % N_MIXERS == 0:
            x2 = _ssd_layer(x2, batch, seq, norm_mix_g[i], ssd_in_w[j], ssd_conv_w[j], ssd_conv_b[j],
                            ssd_dt_bias[j], ssd_a_log[j], ssd_d[j], ssd_norm_g[j], ssd_out_w[j])
        else:
            lambda_init = 0.8 - 0.6 * math.exp(-0.3 * i)
            x2 = _attn_layer(x2, batch, seq, tables, norm_mix_g[i], attn_in_w[j], attn_q_norm_g[j],
                             attn_k_norm_g[j], attn_lq1[j], attn_lk1[j], attn_lq2[j], attn_lk2[j],
                             attn_subln_g[j], attn_out_w[j], lambda_init)
        x2 = _ffn_layer(x2, seq, norm_ffn_g[i], ffn_up_w[i], ffn_conv_w[i], ffn_conv_b[i], ffn_down_w[i])
    return x2.reshape(batch, seq, d)
```

```python
import functools
import math

import jax
import jax.numpy as jnp
from jax import lax
from jax.experimental import pallas as pl
from jax.experimental.pallas import tpu as pltpu

F32 = jnp.float32
BF16 = jnp.bfloat16

V7X_LANES = 128
V7X_SUBLANES = 8
V7X_BF16_SUBLANES = 16
V7X_VMEM_LIMIT_BYTES = 56 * 1024 * 1024

D_MODEL = 1024
CHUNK = 64
EPS = 1e-6
SSD_D_INNER = 2048
SSD_HEAD_DIM = 64
SSD_HEADS = 32
SSD_GROUPS = 4
SSD_HEADS_PER_GROUP = 8
SSD_STATE = 128
SSD_CONV = 4
SSD_CONV_CH = SSD_D_INNER + 2 * SSD_GROUPS * SSD_STATE
ATTN_HEADS = 8
ATTN_HEAD_DIM = 64
ATTN_V_DIM = 128
ATTN_V_ROWS = ATTN_V_DIM + V7X_BF16_SUBLANES
LOG2_E = math.log2(math.e)
ROPE_THETA = 500000.0
ROPE_DIM = 16
D_FF = 2816
FFN_CONV = 3
N_MIXERS = 2

ROW_TILE = 512
SSD_SCAN_CHUNK = 128
SSD_CONV_COLS = 512
SSD_OUT_ROWS = 256
FFN_COLS = 256
ATTN_TILE = 256


def _dot(a, b):
    return jnp.dot(a, b, preferred_element_type=F32)


def _dot_nt(a, b):
    return lax.dot_general(a, b, (((1,), (1,)), ((), ())), preferred_element_type=F32)


def _dot_tn(a, b):
    return lax.dot_general(a, b, (((0,), (0,)), ((), ())), preferred_element_type=F32)


def _silu(x):
    h = 0.5 * x
    return h + h * jnp.tanh(h)


def _rmsnorm(x, g):
    return x * lax.rsqrt(jnp.mean(x * x, axis=-1, keepdims=True) + EPS) * g


def _conv_store(buf_ref, c0, a):
    for j in range(a.shape[1] // V7X_LANES):
        buf_ref[c0 // V7X_LANES + j, V7X_SUBLANES:, :] = a[:, j * V7X_LANES:(j + 1) * V7X_LANES]


def _causal_conv(buf_ref, slab, w, b):
    kw = w.shape[0]
    tm = buf_ref.shape[1] - V7X_SUBLANES
    y = b + w[kw - 1:kw] * buf_ref[slab, V7X_SUBLANES:, :]
    for j in range(1, kw):
        y = y + w[kw - 1 - j:kw - j] * buf_ref[slab, V7X_SUBLANES - j:V7X_SUBLANES - j + tm, :]
    return y


def _conv_history_reset(buf_ref, first):
    @pl.when(first)
    def _():
        buf_ref[:, :V7X_SUBLANES, :] = jnp.zeros((buf_ref.shape[0], V7X_SUBLANES, V7X_LANES), buf_ref.dtype)


def _conv_history_carry(buf_ref):
    tm = buf_ref.shape[1] - V7X_SUBLANES
    buf_ref[:, :V7X_SUBLANES, :] = buf_ref[:, tm:, :]


def _const_spec(shape):
    nd = len(shape)
    return pl.BlockSpec(shape, lambda *_: (0,) * nd, pipeline_mode=pl.Buffered(1))


def _params(semantics):
    return pltpu.CompilerParams(dimension_semantics=semantics,
                                vmem_limit_bytes=V7X_VMEM_LIMIT_BYTES)


def _rope_kernel(pos_ref, freq_ref, m1_ref, m2_ref, c_ref, s1_ref, s2_ref):
    ang = pos_ref[...].astype(F32) * freq_ref[...]
    s = jnp.sin(ang)
    c_ref[...] = jnp.cos(ang)
    s1_ref[...] = s * m1_ref[...]
    s2_ref[...] = s * m2_ref[...]


def _rope_tables(positions):
    b, s = positions.shape
    m = b * s
    inv_freq = ROPE_THETA ** (-jnp.arange(0, ROPE_DIM, 2, dtype=F32) / ROPE_DIM)
    d = jnp.arange(V7X_LANES) % ATTN_HEAD_DIM
    half = ROPE_DIM // 2
    freq = jnp.where(d < ROPE_DIM, inv_freq[d % half], 0.0).astype(F32)[None, :]
    m1 = jnp.where(d < half, -1.0, 0.0).astype(F32)[None, :]
    m2 = jnp.where((d >= half) & (d < ROPE_DIM), 1.0, 0.0).astype(F32)[None, :]
    pos = jnp.broadcast_to(positions.reshape(m, 1), (m, V7X_LANES))
    row = pl.BlockSpec((s, V7X_LANES), lambda i: (i, 0))
    vec = pl.BlockSpec((1, V7X_LANES), lambda i: (0, 0))
    out = jax.ShapeDtypeStruct((m, V7X_LANES), F32)
    return pl.pallas_call(
        _rope_kernel, grid=(b,), in_specs=[row, vec, vec, vec], out_specs=[row, row, row],
        out_shape=[out, out, out], compiler_params=_params(("arbitrary",)), name="rope_tables",
    )(pos, freq, m1, m2)


def _ssd_in_kernel(x_ref, g_ref, wzdt_ref, wxbc_ref, cw_ref, cb_ref, dtb_ref,
                   z_ref, xs_ref, b_ref, c_ref, dt_ref, buf_ref, *, tiles_per_seq):
    _conv_history_reset(buf_ref, pl.program_id(0) % tiles_per_seq == 0)
    h = _rmsnorm(x_ref[...], g_ref[...]).astype(BF16)
    gn = SSD_GROUPS * SSD_STATE
    n_chunks = SSD_CONV_CH // SSD_CONV_COLS
    zw = SSD_CONV_COLS

    def project_z(c):
        if (c + 1) * zw <= SSD_D_INNER:
            z_ref[:, c * zw:(c + 1) * zw] = _dot(h, wzdt_ref[:, c * zw:(c + 1) * zw]).astype(BF16)
        elif c * zw == SSD_D_INNER:
            dt = _dot(h, wzdt_ref[:, SSD_D_INNER:]) + dtb_ref[...]
            dt_ref[...] = jnp.maximum(dt, 0.0) + jnp.log1p(jnp.exp(-jnp.abs(dt)))

    _conv_store(buf_ref, 0, _dot(h, wxbc_ref[:, :SSD_CONV_COLS]))
    for c in range(n_chunks):
        c0 = c * SSD_CONV_COLS
        if c + 1 < n_chunks:
            _conv_store(buf_ref, c0 + SSD_CONV_COLS,
                        _dot(h, wxbc_ref[:, c0 + SSD_CONV_COLS:c0 + 2 * SSD_CONV_COLS]))
        project_z(c)
        for l0 in range(c0, c0 + SSD_CONV_COLS, V7X_LANES):
            lanes = slice(l0, l0 + V7X_LANES)
            y = _silu(_causal_conv(buf_ref, l0 // V7X_LANES, cw_ref[:, lanes], cb_ref[:, lanes])).astype(BF16)
            if l0 < SSD_D_INNER:
                xs_ref[:, lanes] = y
            elif l0 < SSD_D_INNER + gn:
                b_ref[:, l0 - SSD_D_INNER:l0 - SSD_D_INNER + V7X_LANES] = y
            else:
                c_ref[:, l0 - SSD_D_INNER - gn:l0 - SSD_D_INNER - gn + V7X_LANES] = y
    _conv_history_carry(buf_ref)


def _ssd_core_kernel(xs_ref, b_ref, c_ref, dt_ref, z_ref, x_ref, alog_ref, dskip_ref, e_ref,
                     ng_ref, wo_ref, o_ref, h_ref, y_ref):
    ts = xs_ref.shape[0]
    L = SSD_SCAN_CHUNK
    P2 = 2 * SSD_HEAD_DIM
    GP = SSD_HEADS_PER_GROUP * SSD_HEAD_DIM

    @pl.when(pl.program_id(1) == 0)
    def _():
        h_ref[...] = jnp.zeros_like(h_ref)

    row = lax.broadcasted_iota(jnp.int32, (L, L), 0)
    col = lax.broadcasted_iota(jnp.int32, (L, L), 1)
    causal = row >= col
    tril = jnp.where(causal, 1.0, 0.0).astype(BF16)
    left = lax.broadcasted_iota(jnp.int32, (L, P2), 1) < SSD_HEAD_DIM
    a_row = -jnp.exp(alog_ref[...])

    def chunk_body(ci, carry):
        rows = pl.ds(pl.multiple_of(ci * L, L), L)
        dt = dt_ref[rows, :]
        da = dt * a_row
        hi = da.astype(BF16)
        r1 = da - hi.astype(F32)
        mid = r1.astype(BF16)
        lo = (r1 - mid.astype(F32)).astype(BF16)
        cs = _dot(tril, hi) + _dot(tril, mid) + _dot(tril, lo)
        cs_t = cs.T
        ecs = jnp.exp(cs)
        dec = jnp.exp(cs[L - 1:L, :] - cs)
        stacked = jnp.concatenate([dt, dec, ecs], axis=0).astype(BF16)
        wide = _dot(stacked, e_ref[...])
        dt_e, dec_e, ecs_e = wide[:L], wide[L:2 * L], wide[2 * L:]
        xs = xs_ref[rows, :].astype(F32)
        xdt = xs * dt_e
        xdt_b = xdt.astype(BF16)
        dx_b = (xdt * dec_e).astype(BF16)
        for g in range(SSD_GROUPS):
            gc = slice(g * SSD_STATE, (g + 1) * SSD_STATE)
            hc = slice(g * GP, (g + 1) * GP)
            bg = b_ref[rows, gc]
            cg = c_ref[rows, gc]
            cb = _dot_nt(cg, bg)
            hst = h_ref[g]
            y_off = _dot(cg, hst.astype(BF16)) * ecs_e[:, hc]
            h_ref[g] = hst * ecs_e[L - 1:L, hc] + _dot_tn(bg, dx_b[:, hc])
            for p in range(SSD_HEADS_PER_GROUP // 2):
                h0 = g * SSD_HEADS_PER_GROUP + 2 * p
                pc = slice(h0 * SSD_HEAD_DIM, h0 * SSD_HEAD_DIM + P2)
                pair = xdt_b[:, pc]

                def lmat(hd):
                    seg = cs[:, hd:hd + 1] - cs_t[hd:hd + 1, :]
                    return (cb * jnp.exp(jnp.where(causal, seg, -jnp.inf))).astype(BF16)

                lhs = jnp.concatenate([lmat(h0), lmat(h0 + 1)], axis=1)
                rhs = jnp.concatenate([jnp.where(left, pair, 0), jnp.where(left, 0, pair)], axis=0)
                y_ref[rows, pc] = (_dot(lhs, rhs) + y_off[:, p * P2:(p + 1) * P2]
                                   + xs[:, pc] * dskip_ref[:, pc])
        return carry

    lax.fori_loop(0, ts // L, chunk_body, 0, unroll=True)

    gw = SSD_D_INNER // SSD_GROUPS
    for r0 in range(0, ts, SSD_OUT_ROWS):
        rows = slice(r0, r0 + SSD_OUT_ROWS)
        gated = y_ref[rows, :] * _silu(z_ref[rows, :].astype(F32))
        parts = []
        for g in range(SSD_GROUPS):
            gg = gated[:, g * gw:(g + 1) * gw]
            parts.append(gg * lax.rsqrt(jnp.mean(gg * gg, axis=-1, keepdims=True) + EPS))
        yn = (jnp.concatenate(parts, axis=-1) * ng_ref[...]).astype(BF16)
        o_ref[rows, :] = x_ref[rows, :] + _dot(yn, wo_ref[...])


def _ssd_layer(x2, batch, seq, norm_g, in_w, conv_w, conv_b, dt_bias, a_log, d_skip, ssd_norm_g, out_w):
    m = x2.shape[0]
    tm = ROW_TILE
    gn = SSD_GROUPS * SSD_STATE
    wxbc = in_w[:, SSD_D_INNER:SSD_D_INNER + SSD_CONV_CH].astype(BF16)
    pad = V7X_LANES - SSD_HEADS
    wzdt = jnp.pad(jnp.concatenate([in_w[:, :SSD_D_INNER], in_w[:, SSD_D_INNER + SSD_CONV_CH:]], axis=1),
                   ((0, 0), (0, pad))).astype(BF16)
    dtb = jnp.pad(dt_bias, (0, pad))[None, :]
    alog = jnp.pad(a_log, (0, pad))[None, :]

    def rows(width):
        return pl.BlockSpec((tm, width), lambda i: (i, 0))

    z, xs, bm, cm, dt = pl.pallas_call(
        functools.partial(_ssd_in_kernel, tiles_per_seq=seq // tm),
        grid=(m // tm,),
        in_specs=[rows(D_MODEL), _const_spec((1, D_MODEL)), _const_spec(wzdt.shape), _const_spec(wxbc.shape),
                  _const_spec(conv_w.shape), _const_spec((1, SSD_CONV_CH)), _const_spec((1, V7X_LANES))],
        out_specs=[rows(SSD_D_INNER), rows(SSD_D_INNER), rows(gn), rows(gn), rows(V7X_LANES)],
        out_shape=[jax.ShapeDtypeStruct((m, SSD_D_INNER), BF16), jax.ShapeDtypeStruct((m, SSD_D_INNER), BF16),
                   jax.ShapeDtypeStruct((m, gn), BF16), jax.ShapeDtypeStruct((m, gn), BF16),
                   jax.ShapeDtypeStruct((m, V7X_LANES), F32)],
        scratch_shapes=[pltpu.VMEM((SSD_CONV_CH // V7X_LANES, V7X_SUBLANES + tm, V7X_LANES), F32)],
        compiler_params=_params(("arbitrary",)), name="ssd_in",
    )(x2, norm_g[None, :], wzdt, wxbc, conv_w, conv_b[None, :], dtb)

    expand = (jnp.arange(V7X_LANES)[:, None] == (jnp.arange(SSD_D_INNER) // SSD_HEAD_DIM)[None, :]).astype(BF16)
    dskip = jnp.repeat(d_skip, SSD_HEAD_DIM)[None, :]
    ts = ROW_TILE
    spt = seq // ts

    def tile(width):
        return pl.BlockSpec((ts, width), lambda b, s: (b * spt + s, 0))

    return pl.pallas_call(
        _ssd_core_kernel,
        grid=(batch, spt),
        in_specs=[tile(SSD_D_INNER), tile(gn), tile(gn), tile(V7X_LANES), tile(SSD_D_INNER), tile(D_MODEL),
                  _const_spec((1, V7X_LANES)), _const_spec((1, SSD_D_INNER)), _const_spec(expand.shape),
                  _const_spec((1, SSD_D_INNER)), _const_spec((SSD_D_INNER, D_MODEL))],
        out_specs=tile(D_MODEL),
        out_shape=jax.ShapeDtypeStruct((m, D_MODEL), F32),
        scratch_shapes=[pltpu.VMEM((SSD_GROUPS, SSD_STATE, SSD_HEADS_PER_GROUP * SSD_HEAD_DIM), F32),
                        pltpu.VMEM((ts, SSD_D_INNER), F32)],
        compiler_params=_params(("arbitrary", "arbitrary")), name="ssd_core",
    )(xs, bm, cm, dt, z, x2, alog, dskip, expand, ssd_norm_g[None, :], out_w.astype(BF16))


def _attn_in_kernel(x_ref, g_ref, wqk_ref, wvt_ref, c_ref, s1_ref, s2_ref, qg_ref, kg_ref, seg_ref,
                    q_ref, k_ref, vt_ref):
    t = ATTN_TILE
    hw = 2 * ATTN_HEAD_DIM
    half = ROPE_DIM // 2
    tm = x_ref.shape[0]
    h = _rmsnorm(x_ref[...], g_ref[...]).astype(BF16)
    cos_t, sin1_t, sin2_t = c_ref[...], s1_ref[...], s2_ref[...]
    q_scale = LOG2_E / math.sqrt(ATTN_HEAD_DIM)
    for c0, o_ref, gain_ref, scale in ((0, q_ref, qg_ref, q_scale), (D_MODEL, k_ref, kg_ref, None)):
        a = _dot(h, wqk_ref[:, c0:c0 + D_MODEL])
        for p0 in range(0, D_MODEL, 2 * hw):
            pair = a[:, p0:p0 + 2 * hw]
            ss = _dot((pair * pair).astype(BF16), seg_ref[...])
            inv = lax.rsqrt(ss * (1.0 / ATTN_HEAD_DIM) + EPS)
            for j in range(2):
                lanes = slice(j * hw, (j + 1) * hw)
                xn = pair[:, lanes] * inv[:, lanes] * gain_ref[...]
                y = (xn * cos_t + pltpu.roll(xn, hw - half, axis=1) * sin1_t
                     + pltpu.roll(xn, half, axis=1) * sin2_t)
                o_ref[:, p0 + j * hw:p0 + (j + 1) * hw] = (y if scale is None else y * scale).astype(BF16)
    vt = _dot_nt(wvt_ref[...], h).astype(BF16)
    ones = jnp.ones((ATTN_V_ROWS - ATTN_V_DIM, tm), BF16)
    pieces = []
    for g in range(ATTN_HEADS):
        pieces += [vt[g * ATTN_V_DIM:(g + 1) * ATTN_V_DIM], ones]
    vt = jnp.concatenate(pieces, axis=0)
    for j in range(tm // t):
        vt_ref[j] = vt[:, j * t:(j + 1) * t]


def _flash_kernel(q_ref, k_ref, vt_ref, x_ref, lq1_ref, lk1_ref, lq2_ref, lk2_ref, sg_ref, wo_ref,
                  o_ref, acc_ref, *, lambda_init):
    t = ATTN_TILE
    hw = 2 * ATTN_HEAD_DIM
    qi = pl.program_id(1)
    left_row = lax.broadcasted_iota(jnp.int32, (1, V7X_LANES), 1) < ATTN_HEAD_DIM
    heads = range(ATTN_HEADS)

    qa, qb = [], []
    for g in heads:
        qh = q_ref[:, g * hw:(g + 1) * hw]
        qa.append(jnp.where(left_row, qh, 0))
        qb.append(jnp.where(left_row, 0, qh))

    def kv_tile(kt, carry, masked=False):
        rows = pl.ds(pl.multiple_of(kt * t, t), t)
        scores = []
        for g in heads:
            ks = k_ref[rows, g * hw:(g + 1) * hw]
            scores.append((_dot_nt(ks, qa[g]), _dot_nt(ks, qb[g])))
        if masked:
            kc = lax.broadcasted_iota(jnp.int32, (t, t), 0) // CHUNK
            qc = lax.broadcasted_iota(jnp.int32, (t, t), 1) // CHUNK
            allowed = kc <= qc
            scores = [tuple(jnp.where(allowed, s, -jnp.inf) for s in pair) for pair in scores]
        probs, out = [], []
        for g in heads:
            new, alphas, ps = [], [], []
            for c, s in enumerate(scores[g]):
                m = jnp.max(s, axis=0, keepdims=True)
                if carry is not None:
                    m_old = carry[g][c]
                    m = jnp.maximum(m_old, m)
                    alphas.append(jnp.exp2(m_old - m))
                ps.append(jnp.exp2(s - m))
                new.append(m)
            probs.append((jnp.concatenate(ps, axis=1).astype(BF16), alphas))
            out.append(tuple(new))
        for g in heads:
            p, alphas = probs[g]
            pv = _dot(vt_ref[kt, g * ATTN_V_ROWS:(g + 1) * ATTN_V_ROWS, :], p)
            if carry is None:
                acc_ref[g] = pv
            else:
                acc_ref[g] = acc_ref[g] * jnp.concatenate(alphas, axis=1) + pv
        return tuple(out)

    lax.fori_loop(0, qi, kv_tile, kv_tile(qi, None, True))

    lam = (jnp.exp(jnp.sum(lq1_ref[...] * lk1_ref[...], axis=-1, keepdims=True))
           - jnp.exp(jnp.sum(lq2_ref[...] * lk2_ref[...], axis=-1, keepdims=True)) + lambda_init)
    outs = []
    for g in heads:
        acc = acc_ref[g]
        la = acc[ATTN_V_DIM:ATTN_V_DIM + 1, :t]
        lb = acc[ATTN_V_DIM:ATTN_V_DIM + 1, t:]
        ot = acc[:ATTN_V_DIM, :t] / la - lam * (acc[:ATTN_V_DIM, t:] / lb)
        ot = ot * lax.rsqrt(jnp.mean(ot * ot, axis=0, keepdims=True) + EPS)
        outs.append((ot.T * sg_ref[...] * (1.0 - lambda_init)).astype(BF16))
    o_ref[...] = x_ref[...] + _dot(jnp.concatenate(outs, axis=1), wo_ref[...])


def _attn_layer(x2, batch, seq, tables, norm_g, in_w, q_norm_g, k_norm_g, lq1, lk1, lq2, lk2,
                subln_g, out_w, lambda_init):
    m = x2.shape[0]
    tm = ROW_TILE
    tk = tq = ATTN_TILE
    qt = seq // tq
    hw = 2 * ATTN_HEAD_DIM
    wqk = in_w[:, :2 * D_MODEL].astype(BF16)
    wvt = in_w[:, 2 * D_MODEL:].T.astype(BF16)
    cos_t, sin1_t, sin2_t = tables
    lane = jnp.arange(2 * hw)
    seg = (lane[:, None] // ATTN_HEAD_DIM == lane[None, :] // ATTN_HEAD_DIM).astype(BF16)
    rows_spec = pl.BlockSpec((tm, D_MODEL), lambda i: (i, 0))
    table_spec = pl.BlockSpec((tm, V7X_LANES), lambda i: (i, 0))
    qkv_shape = jax.ShapeDtypeStruct((m, D_MODEL), BF16)
    q, k, vt = pl.pallas_call(
        _attn_in_kernel,
        grid=(m // tm,),
        in_specs=[rows_spec, _const_spec((1, D_MODEL)), _const_spec(wqk.shape), _const_spec(wvt.shape),
                  table_spec, table_spec, table_spec, _const_spec((1, hw)), _const_spec((1, hw)),
                  _const_spec(seg.shape)],
        out_specs=[rows_spec, rows_spec,
                   pl.BlockSpec((tm // tk, ATTN_HEADS * ATTN_V_ROWS, tk), lambda i: (i, 0, 0))],
        out_shape=[qkv_shape, qkv_shape,
                   jax.ShapeDtypeStruct((m // tk, ATTN_HEADS * ATTN_V_ROWS, tk), BF16)],
        compiler_params=_params(("arbitrary",)), name="attn_in",
    )(x2, norm_g[None, :], wqk, wvt, cos_t, sin1_t, sin2_t, jnp.tile(q_norm_g, 2)[None, :],
      jnp.tile(k_norm_g, 2)[None, :], seg)

    vec64 = _const_spec((1, ATTN_HEAD_DIM))
    tile_spec = pl.BlockSpec((tq, D_MODEL), lambda b, qi: (b * qt + qi, 0))
    return pl.pallas_call(
        functools.partial(_flash_kernel, lambda_init=lambda_init),
        grid=(batch, qt),
        in_specs=[tile_spec,
                  pl.BlockSpec((seq, D_MODEL), lambda b, qi: (b, 0)),
                  pl.BlockSpec((seq // tk, ATTN_HEADS * ATTN_V_ROWS, tk), lambda b, qi: (b, 0, 0)),
                  tile_spec, vec64, vec64, vec64, vec64, _const_spec((1, ATTN_V_DIM)),
                  _const_spec((D_MODEL, D_MODEL))],
        out_specs=tile_spec,
        out_shape=jax.ShapeDtypeStruct((m, D_MODEL), F32),
        scratch_shapes=[pltpu.VMEM((ATTN_HEADS, ATTN_V_ROWS, 2 * tq), F32)],
        compiler_params=_params(("arbitrary", "arbitrary")), name="flash",
    )(q, k, vt, x2, lq1[None, :], lk1[None, :], lq2[None, :], lk2[None, :], subln_g[None, :],
      out_w.astype(BF16))


def _ffn_kernel(x_ref, g_ref, wg_ref, wu_ref, cwg_ref, cwu_ref, cbg_ref, cbu_ref, wd_ref,
                o_ref, buf_g_ref, buf_u_ref, act_ref, *, tiles_per_seq):
    first = pl.program_id(0) % tiles_per_seq == 0
    _conv_history_reset(buf_g_ref, first)
    _conv_history_reset(buf_u_ref, first)
    x = x_ref[...]
    h = _rmsnorm(x, g_ref[...]).astype(BF16)
    for c0 in range(0, D_FF, FFN_COLS):
        _conv_store(buf_g_ref, c0, _dot(h, wg_ref[:, c0:c0 + FFN_COLS]))
        _conv_store(buf_u_ref, c0, _dot(h, wu_ref[:, c0:c0 + FFN_COLS]))
        for l0 in range(c0, c0 + FFN_COLS, V7X_LANES):
            lanes = slice(l0, l0 + V7X_LANES)
            yg = _causal_conv(buf_g_ref, l0 // V7X_LANES, cwg_ref[:, lanes], cbg_ref[:, lanes])
            yu = _causal_conv(buf_u_ref, l0 // V7X_LANES, cwu_ref[:, lanes], cbu_ref[:, lanes])
            act_ref[:, lanes] = (_silu(yg) * yu).astype(BF16)
    _conv_history_carry(buf_g_ref)
    _conv_history_carry(buf_u_ref)
    o_ref[...] = x + _dot(act_ref[...], wd_ref[...])


def _ffn_layer(x2, seq, norm_g, up_w, conv_w, conv_b, down_w):
    m = x2.shape[0]
    tm = ROW_TILE
    wg = up_w[:, :D_FF].astype(BF16)
    wu = up_w[:, D_FF:].astype(BF16)
    row = pl.BlockSpec((tm, D_MODEL), lambda i: (i, 0))
    return pl.pallas_call(
        functools.partial(_ffn_kernel, tiles_per_seq=seq // tm),
        grid=(m // tm,),
        in_specs=[row, _const_spec((1, D_MODEL)), _const_spec((D_MODEL, D_FF)), _const_spec((D_MODEL, D_FF)),
                  _const_spec((FFN_CONV, D_FF)), _const_spec((FFN_CONV, D_FF)), _const_spec((1, D_FF)),
                  _const_spec((1, D_FF)), _const_spec((D_FF, D_MODEL))],
        out_specs=row,
        out_shape=jax.ShapeDtypeStruct((m, D_MODEL), F32),
        scratch_shapes=[pltpu.VMEM((D_FF // V7X_LANES, V7X_SUBLANES + tm, V7X_LANES), F32),
                        pltpu.VMEM((D_FF // V7X_LANES, V7X_SUBLANES + tm, V7X_LANES), F32),
                        pltpu.VMEM((tm, D_FF), BF16)],
        compiler_params=_params(("arbitrary",)), name="ffn",
    )(x2, norm_g[None, :], wg, wu, conv_w[:, :D_FF], conv_w[:, D_FF:], conv_b[None, :D_FF],
      conv_b[None, D_FF:], down_w.astype(BF16))


def kernel(x, positions, norm_mix_g, norm_ffn_g, ssd_in_w, ssd_conv_w, ssd_conv_b, ssd_dt_bias, ssd_a_log, ssd_d, ssd_norm_g, ssd_out_w, attn_in_w, attn_q_norm_g, attn_k_norm_g, attn_lq1, attn_lk1, attn_lq2, attn_lk2, attn_subln_g, attn_out_w, ffn_up_w, ffn_conv_w, ffn_conv_b, ffn_down_w):
    batch, seq, d = x.shape
    depth = norm_mix_g.shape[0]
    assert d == D_MODEL and seq % ROW_TILE == 0 and ROW_TILE % ATTN_TILE == 0 and ATTN_TILE % CHUNK == 0
    x2 = x.reshape(batch * seq, d)
    tables = _rope_tables(positions)
    for i in range(depth):
        j = i // N_MIXERS
        if i % N_MIXERS == 0:
            x2 = _ssd_layer(x2, batch, seq, norm_mix_g[i], ssd_in_w[j], ssd_conv_w[j], ssd_conv_b[j],
                            ssd_dt_bias[j], ssd_a_log[j], ssd_d[j], ssd_norm_g[j], ssd_out_w[j])
        else:
            lambda_init = 0.8 - 0.6 * math.exp(-0.3 * i)
            x2 = _attn_layer(x2, batch, seq, tables, norm_mix_g[i], attn_in_w[j], attn_q_norm_g[j],
                             attn_k_norm_g[j], attn_lq1[j], attn_lk1[j], attn_lq2[j], attn_lk2[j],
                             attn_subln_g[j], attn_out_w[j], lambda_init)
        x2 = _ffn_layer(x2, seq, norm_ffn_g[i], ffn_up_w[i], ffn_conv_w[i], ffn_conv_b[i], ffn_down_w[i])
    return x2.reshape(batch, seq, d)
```

```python
import functools
import math

import jax
import jax.numpy as jnp
from jax import lax
from jax.experimental import pallas as pl
from jax.experimental.pallas import tpu as pltpu

F32 = jnp.float32
BF16 = jnp.bfloat16

V7X_LANES = 128
V7X_SUBLANES = 8
V7X_BF16_SUBLANES = 16
V7X_VMEM_LIMIT_BYTES = 56 * 1024 * 1024

D_MODEL = 1024
CHUNK = 64
EPS = 1e-6
SSD_D_INNER = 2048
SSD_HEAD_DIM = 64
SSD_HEADS = 32
SSD_GROUPS = 4
SSD_HEADS_PER_GROUP = 8
SSD_STATE = 128
SSD_CONV = 4
SSD_CONV_CH = SSD_D_INNER + 2 * SSD_GROUPS * SSD_STATE
ATTN_HEADS = 8
ATTN_HEAD_DIM = 64
ATTN_V_DIM = 128
ATTN_V_ROWS = ATTN_V_DIM + V7X_BF16_SUBLANES
LOG2_E = math.log2(math.e)
ROPE_THETA = 500000.0
ROPE_DIM = 16
D_FF = 2816
FFN_CONV = 3
N_MIXERS = 2

ROW_TILE = 512
SSD_SCAN_CHUNK = 128
SSD_CONV_COLS = 512
SSD_OUT_ROWS = 256
FFN_COLS = 256
ATTN_TILE = 256


def _dot(a, b):
    return jnp.dot(a, b, preferred_element_type=F32)


def _dot_nt(a, b):
    return lax.dot_general(a, b, (((1,), (1,)), ((), ())), preferred_element_type=F32)


def _dot_tn(a, b):
    return lax.dot_general(a, b, (((0,), (0,)), ((), ())), preferred_element_type=F32)


def _silu(x):
    h = 0.5 * x
    return h + h * jnp.tanh(h)


def _rmsnorm(x, g):
    return x * lax.rsqrt(jnp.mean(x * x, axis=-1, keepdims=True) + EPS) * g


def _conv_store(buf_ref, c0, a):
    for j in range(a.shape[1] // V7X_LANES):
        buf_ref[c0 // V7X_LANES + j, V7X_SUBLANES:, :] = a[:, j * V7X_LANES:(j + 1) * V7X_LANES]


def _causal_conv(buf_ref, slab, w, b):
    kw = w.shape[0]
    tm = buf_ref.shape[1] - V7X_SUBLANES
    y = b + w[kw - 1:kw] * buf_ref[slab, V7X_SUBLANES:, :]
    for j in range(1, kw):
        y = y + w[kw - 1 - j:kw - j] * buf_ref[slab, V7X_SUBLANES - j:V7X_SUBLANES - j + tm, :]
    return y


def _conv_history_reset(buf_ref, first):
    @pl.when(first)
    def _():
        buf_ref[:, :V7X_SUBLANES, :] = jnp.zeros((buf_ref.shape[0], V7X_SUBLANES, V7X_LANES), buf_ref.dtype)


def _conv_history_carry(buf_ref):
    tm = buf_ref.shape[1] - V7X_SUBLANES
    buf_ref[:, :V7X_SUBLANES, :] = buf_ref[:, tm:, :]


def _const_spec(shape):
    nd = len(shape)
    return pl.BlockSpec(shape, lambda *_: (0,) * nd, pipeline_mode=pl.Buffered(1))


def _params(semantics):
    return pltpu.CompilerParams(dimension_semantics=semantics,
                                vmem_limit_bytes=V7X_VMEM_LIMIT_BYTES)


def _rope_kernel(pos_ref, freq_ref, m1_ref, m2_ref, c_ref, s1_ref, s2_ref):
    ang = pos_ref[...].astype(F32) * freq_ref[...]
    s = jnp.sin(ang)
    c_ref[...] = jnp.cos(ang)
    s1_ref[...] = s * m1_ref[...]
    s2_ref[...] = s * m2_ref[...]


def _rope_tables(positions):
    b, s = positions.shape
    m = b * s
    inv_freq = ROPE_THETA ** (-jnp.arange(0, ROPE_DIM, 2, dtype=F32) / ROPE_DIM)
    d = jnp.arange(V7X_LANES) % ATTN_HEAD_DIM
    half = ROPE_DIM // 2
    freq = jnp.where(d < ROPE_DIM, inv_freq[d % half], 0.0).astype(F32)[None, :]
    m1 = jnp.where(d < half, -1.0, 0.0).astype(F32)[None, :]
    m2 = jnp.where((d >= half) & (d < ROPE_DIM), 1.0, 0.0).astype(F32)[None, :]
    pos = jnp.broadcast_to(positions.reshape(m, 1), (m, V7X_LANES))
    row = pl.BlockSpec((s, V7X_LANES), lambda i: (i, 0))
    vec = pl.BlockSpec((1, V7X_LANES), lambda i: (0, 0))
    out = jax.ShapeDtypeStruct((m, V7X_LANES), F32)
    return pl.pallas_call(
        _rope_kernel, grid=(b,), in_specs=[row, vec, vec, vec], out_specs=[row, row, row],
        out_shape=[out, out, out], compiler_params=_params(("arbitrary",)), name="rope_tables",
    )(pos, freq, m1, m2)


def _ssd_in_kernel(x_ref, g_ref, wzdt_ref, wxbc_ref, cw_ref, cb_ref, dtb_ref,
                   z_ref, xs_ref, b_ref, c_ref, dt_ref, buf_ref, *, tiles_per_seq):
    _conv_history_reset(buf_ref, pl.program_id(0) % tiles_per_seq == 0)
    h = _rmsnorm(x_ref[...], g_ref[...]).astype(BF16)
    gn = SSD_GROUPS * SSD_STATE
    n_chunks = SSD_CONV_CH // SSD_CONV_COLS
    zw = SSD_CONV_COLS

    def project_z(c):
        if (c + 1) * zw <= SSD_D_INNER:
            z_ref[:, c * zw:(c + 1) * zw] = _dot(h, wzdt_ref[:, c * zw:(c + 1) * zw]).astype(BF16)
        elif c * zw == SSD_D_INNER:
            dt = _dot(h, wzdt_ref[:, SSD_D_INNER:]) + dtb_ref[...]
            dt_ref[...] = jnp.maximum(dt, 0.0) + jnp.log1p(jnp.exp(-jnp.abs(dt)))

    _conv_store(buf_ref, 0, _dot(h, wxbc_ref[:, :SSD_CONV_COLS]))
    for c in range(n_chunks):
        c0 = c * SSD_CONV_COLS
        if c + 1 < n_chunks:
            _conv_store(buf_ref, c0 + SSD_CONV_COLS,
                        _dot(h, wxbc_ref[:, c0 + SSD_CONV_COLS:c0 + 2 * SSD_CONV_COLS]))
        project_z(c)
        for l0 in range(c0, c0 + SSD_CONV_COLS, V7X_LANES):
            lanes = slice(l0, l0 + V7X_LANES)
            y = _silu(_causal_conv(buf_ref, l0 // V7X_LANES, cw_ref[:, lanes], cb_ref[:, lanes])).astype(BF16)
            if l0 < SSD_D_INNER:
                xs_ref[:, lanes] = y
            elif l0 < SSD_D_INNER + gn:
                b_ref[:, l0 - SSD_D_INNER:l0 - SSD_D_INNER + V7X_LANES] = y
            else:
                c_ref[:, l0 - SSD_D_INNER - gn:l0 - SSD_D_INNER - gn + V7X_LANES] = y
    _conv_history_carry(buf_ref)


def _ssd_core_kernel(xs_ref, b_ref, c_ref, dt_ref, z_ref, x_ref, alog_ref, dskip_ref, e_ref,
                     ng_ref, wo_ref, o_ref, h_ref, y_ref):
    ts = xs_ref.shape[0]
    L = SSD_SCAN_CHUNK
    P2 = 2 * SSD_HEAD_DIM
    GP = SSD_HEADS_PER_GROUP * SSD_HEAD_DIM

    @pl.when(pl.program_id(1) == 0)
    def _():
        h_ref[...] = jnp.zeros_like(h_ref)

    row = lax.broadcasted_iota(jnp.int32, (L, L), 0)
    col = lax.broadcasted_iota(jnp.int32, (L, L), 1)
    causal = row >= col
    tril = jnp.where(causal, 1.0, 0.0).astype(BF16)
    left = lax.broadcasted_iota(jnp.int32, (L, P2), 1) < SSD_HEAD_DIM
    a_row = -jnp.exp(alog_ref[...])

    def chunk_body(ci, carry):
        rows = pl.ds(pl.multiple_of(ci * L, L), L)
        dt = dt_ref[rows, :]
        da = dt * a_row
        hi = da.astype(BF16)
        r1 = da - hi.astype(F32)
        mid = r1.astype(BF16)
        lo = (r1 - mid.astype(F32)).astype(BF16)
        cs = _dot(tril, hi) + _dot(tril, mid) + _dot(tril, lo)
        cs_t = cs.T
        ecs = jnp.exp(cs)
        dec = jnp.exp(cs[L - 1:L, :] - cs)
        stacked = jnp.concatenate([dt, dec, ecs], axis=0).astype(BF16)
        wide = _dot(stacked, e_ref[...])
        dt_e, dec_e, ecs_e = wide[:L], wide[L:2 * L], wide[2 * L:]
        xs = xs_ref[rows, :].astype(F32)
        xdt = xs * dt_e
        xdt_b = xdt.astype(BF16)
        dx_b = (xdt * dec_e).astype(BF16)
        for g in range(SSD_GROUPS):
            gc = slice(g * SSD_STATE, (g + 1) * SSD_STATE)
            hc = slice(g * GP, (g + 1) * GP)
            bg = b_ref[rows, gc]
            cg = c_ref[rows, gc]
            cb = _dot_nt(cg, bg)
            hst = h_ref[g]
            y_off = _dot(cg, hst.astype(BF16)) * ecs_e[:, hc]
            h_ref[g] = hst * ecs_e[L - 1:L, hc] + _dot_tn(bg, dx_b[:, hc])
            for p in range(SSD_HEADS_PER_GROUP // 2):
                h0 = g * SSD_HEADS_PER_GROUP + 2 * p
                pc = slice(h0 * SSD_HEAD_DIM, h0 * SSD_HEAD_DIM + P2)
                pair = xdt_b[:, pc]

                def lmat(hd):
                    seg = cs[:, hd:hd + 1] - cs_t[hd:hd + 1, :]
                    return (cb * jnp.exp(jnp.where(causal, seg, -jnp.inf))).astype(BF16)

                lhs = jnp.concatenate([lmat(h0), lmat(h0 + 1)], axis=1)
                rhs = jnp.concatenate([jnp.where(left, pair, 0), jnp.where(left, 0, pair)], axis=0)
                y_ref[rows, pc] = (_dot(lhs, rhs) + y_off[:, p * P2:(p + 1) * P2]
                                   + xs[:, pc] * dskip_ref[:, pc])
        return carry

    lax.fori_loop(0, ts // L, chunk_body, 0, unroll=True)

    gw = SSD_D_INNER // SSD_GROUPS
    for r0 in range(0, ts, SSD_OUT_ROWS):
        rows = slice(r0, r0 + SSD_OUT_ROWS)
        gated = y_ref[rows, :] * _silu(z_ref[rows, :].astype(F32))
        parts = []
        for g in range(SSD_GROUPS):
            gg = gated[:, g * gw:(g + 1) * gw]
            parts.append(gg * lax.rsqrt(jnp.mean(gg * gg, axis=-1, keepdims=True) + EPS))
        yn = (jnp.concatenate(parts, axis=-1) * ng_ref[...]).astype(BF16)
        o_ref[rows, :] = x_ref[rows, :] + _dot(yn, wo_ref[...])


def _ssd_layer(x2, batch, seq, norm_g, in_w, conv_w, conv_b, dt_bias, a_log, d_skip, ssd_norm_g, out_w):
    m = x2.shape[0]
    tm = ROW_TILE
    gn = SSD_GROUPS * SSD_STATE
    wxbc = in_w[:, SSD_D_INNER:SSD_D_INNER + SSD_CONV_CH].astype(BF16)
    pad = V7X_LANES - SSD_HEADS
    wzdt = jnp.pad(jnp.concatenate([in_w[:, :SSD_D_INNER], in_w[:, SSD_D_INNER + SSD_CONV_CH:]], axis=1),
                   ((0, 0), (0, pad))).astype(BF16)
    dtb = jnp.pad(dt_bias, (0, pad))[None, :]
    alog = jnp.pad(a_log, (0, pad))[None, :]

    def rows(width):
        return pl.BlockSpec((tm, width), lambda i: (i, 0))

    z, xs, bm, cm, dt = pl.pallas_call(
        functools.partial(_ssd_in_kernel, tiles_per_seq=seq // tm),
        grid=(m // tm,),
        in_specs=[rows(D_MODEL), _const_spec((1, D_MODEL)), _const_spec(wzdt.shape), _const_spec(wxbc.shape),
                  _const_spec(conv_w.shape), _const_spec((1, SSD_CONV_CH)), _const_spec((1, V7X_LANES))],
        out_specs=[rows(SSD_D_INNER), rows(SSD_D_INNER), rows(gn), rows(gn), rows(V7X_LANES)],
        out_shape=[jax.ShapeDtypeStruct((m, SSD_D_INNER), BF16), jax.ShapeDtypeStruct((m, SSD_D_INNER), BF16),
                   jax.ShapeDtypeStruct((m, gn), BF16), jax.ShapeDtypeStruct((m, gn), BF16),
                   jax.ShapeDtypeStruct((m, V7X_LANES), F32)],
        scratch_shapes=[pltpu.VMEM((SSD_CONV_CH // V7X_LANES, V7X_SUBLANES + tm, V7X_LANES), F32)],
        compiler_params=_params(("arbitrary",)), name="ssd_in",
    )(x2, norm_g[None, :], wzdt, wxbc, conv_w, conv_b[None, :], dtb)

    expand = (jnp.arange(V7X_LANES)[:, None] == (jnp.arange(SSD_D_INNER) // SSD_HEAD_DIM)[None, :]).astype(BF16)
    dskip = jnp.repeat(d_skip, SSD_HEAD_DIM)[None, :]
    ts = ROW_TILE
    spt = seq // ts

    def tile(width):
        return pl.BlockSpec((ts, width), lambda b, s: (b * spt + s, 0))

    return pl.pallas_call(
        _ssd_core_kernel,
        grid=(batch, spt),
        in_specs=[tile(SSD_D_INNER), tile(gn), tile(gn), tile(V7X_LANES), tile(SSD_D_INNER), tile(D_MODEL),
                  _const_spec((1, V7X_LANES)), _const_spec((1, SSD_D_INNER)), _const_spec(expand.shape),
                  _const_spec((1, SSD_D_INNER)), _const_spec((SSD_D_INNER, D_MODEL))],
        out_specs=tile(D_MODEL),
        out_shape=jax.ShapeDtypeStruct((m, D_MODEL), F32),
        scratch_shapes=[pltpu.VMEM((SSD_GROUPS, SSD_STATE, SSD_HEADS_PER_GROUP * SSD_HEAD_DIM), F32),
                        pltpu.VMEM((ts, SSD_D_INNER), F32)],
        compiler_params=_params(("arbitrary", "arbitrary")), name="ssd_core",
    )(xs, bm, cm, dt, z, x2, alog, dskip, expand, ssd_norm_g[None, :], out_w.astype(BF16))


def _attn_in_kernel(x_ref, g_ref, wqk_ref, wvt_ref, c_ref, s1_ref, s2_ref, qg_ref, kg_ref, seg_ref,
                    q_ref, k_ref, vt_ref):
    t = ATTN_TILE
    hw = 2 * ATTN_HEAD_DIM
    half = ROPE_DIM // 2
    tm = x_ref.shape[0]
    h = _rmsnorm(x_ref[...], g_ref[...]).astype(BF16)
    cos_t, sin1_t, sin2_t = c_ref[...], s1_ref[...], s2_ref[...]
    q_scale = LOG2_E / math.sqrt(ATTN_HEAD_DIM)
    for c0, o_ref, gain_ref, scale in ((0, q_ref, qg_ref, q_scale), (D_MODEL, k_ref, kg_ref, None)):
        a = _dot(h, wqk_ref[:, c0:c0 + D_MODEL])
        for p0 in range(0, D_MODEL, 2 * hw):
            pair = a[:, p0:p0 + 2 * hw]
            ss = _dot((pair * pair).astype(BF16), seg_ref[...])
            inv = lax.rsqrt(ss * (1.0 / ATTN_HEAD_DIM) + EPS)
            for j in range(2):
                lanes = slice(j * hw, (j + 1) * hw)
                xn = pair[:, lanes] * inv[:, lanes] * gain_ref[...]
                y = (xn * cos_t + pltpu.roll(xn, hw - half, axis=1) * sin1_t
                     + pltpu.roll(xn, half, axis=1) * sin2_t)
                o_ref[:, p0 + j * hw:p0 + (j + 1) * hw] = (y if scale is None else y * scale).astype(BF16)
    vt = _dot_nt(wvt_ref[...], h).astype(BF16)
    ones = jnp.ones((ATTN_V_ROWS - ATTN_V_DIM, tm), BF16)
    pieces = []
    for g in range(ATTN_HEADS):
        pieces += [vt[g * ATTN_V_DIM:(g + 1) * ATTN_V_DIM], ones]
    vt = jnp.concatenate(pieces, axis=0)
    for j in range(tm // t):
        vt_ref[j] = vt[:, j * t:(j + 1) * t]


def _flash_kernel(q_ref, k_ref, vt_ref, x_ref, lq1_ref, lk1_ref, lq2_ref, lk2_ref, sg_ref, wo_ref,
                  o_ref, acc_ref, *, lambda_init):
    t = ATTN_TILE
    hw = 2 * ATTN_HEAD_DIM
    qi = pl.program_id(1)
    left_row = lax.broadcasted_iota(jnp.int32, (1, V7X_LANES), 1) < ATTN_HEAD_DIM
    heads = range(ATTN_HEADS)

    qa, qb = [], []
    for g in heads:
        qh = q_ref[:, g * hw:(g + 1) * hw]
        qa.append(jnp.where(left_row, qh, 0))
        qb.append(jnp.where(left_row, 0, qh))

    def kv_tile(kt, carry, masked=False):
        rows = pl.ds(pl.multiple_of(kt * t, t), t)
        scores = []
        for g in heads:
            ks = k_ref[rows, g * hw:(g + 1) * hw]
            scores.append((_dot_nt(ks, qa[g]), _dot_nt(ks, qb[g])))
        if masked:
            kc = lax.broadcasted_iota(jnp.int32, (t, t), 0) // CHUNK
            qc = lax.broadcasted_iota(jnp.int32, (t, t), 1) // CHUNK
            allowed = kc <= qc
            scores = [tuple(jnp.where(allowed, s, -jnp.inf) for s in pair) for pair in scores]
        probs, out = [], []
        for g in heads:
            new, alphas, ps = [], [], []
            for c, s in enumerate(scores[g]):
                m = jnp.max(s, axis=0, keepdims=True)
                if carry is not None:
                    m_old = carry[g][c]
                    m = jnp.maximum(m_old, m)
                    alphas.append(jnp.exp2(m_old - m))
                ps.append(jnp.exp2(s - m))
                new.append(m)
            probs.append((jnp.concatenate(ps, axis=1).astype(BF16), alphas))
            out.append(tuple(new))
        for g in heads:
            p, alphas = probs[g]
            pv = _dot(vt_ref[kt, g * ATTN_V_ROWS:(g + 1) * ATTN_V_ROWS, :], p)
            if carry is None:
                acc_ref[g] = pv
            else:
                acc_ref[g] = acc_ref[g] * jnp.concatenate(alphas, axis=1) + pv
        return tuple(out)

    lax.fori_loop(0, qi, kv_tile, kv_tile(qi, None, True))

    lam = (jnp.exp(jnp.sum(lq1_ref[...] * lk1_ref[...], axis=-1, keepdims=True))
           - jnp.exp(jnp.sum(lq2_ref[...] * lk2_ref[...], axis=-1, keepdims=True)) + lambda_init)
    outs = []
    for g in heads:
        acc = acc_ref[g]
        la = acc[ATTN_V_DIM:ATTN_V_DIM + 1, :t]
        lb = acc[ATTN_V_DIM:ATTN_V_DIM + 1, t:]
        ot = acc[:ATTN_V_DIM, :t] / la - lam * (acc[:ATTN_V_DIM, t:] / lb)
        ot = ot * lax.rsqrt(jnp.mean(ot * ot, axis=0, keepdims=True) + EPS)
        outs.append((ot.T * sg_ref[...] * (1.0 - lambda_init)).astype(BF16))
    o_ref[...] = x_ref[...] + _dot(jnp.concatenate(outs, axis=1), wo_ref[...])


def _attn_layer(x2, batch, seq, tables, norm_g, in_w, q_norm_g, k_norm_g, lq1, lk1, lq2, lk2,
                subln_g, out_w, lambda_init):
    m = x2.shape[0]
    tm = ROW_TILE
    tk = tq = ATTN_TILE
    qt = seq // tq
    hw = 2 * ATTN_HEAD_DIM
    wqk = in_w[:, :2 * D_MODEL].astype(BF16)
    wvt = in_w[:, 2 * D_MODEL:].T.astype(BF16)
    cos_t, sin1_t, sin2_t = tables
    lane = jnp.arange(2 * hw)
    seg = (lane[:, None] // ATTN_HEAD_DIM == lane[None, :] // ATTN_HEAD_DIM).astype(BF16)
    rows_spec = pl.BlockSpec((tm, D_MODEL), lambda i: (i, 0))
    table_spec = pl.BlockSpec((tm, V7X_LANES), lambda i: (i, 0))
    qkv_shape = jax.ShapeDtypeStruct((m, D_MODEL), BF16)
    q, k, vt = pl.pallas_call(
        _attn_in_kernel,
        grid=(m // tm,),
        in_specs=[rows_spec, _const_spec((1, D_MODEL)), _const_spec(wqk.shape), _const_spec(wvt.shape),
                  table_spec, table_spec, table_spec, _const_spec((1, hw)), _const_spec((1, hw)),
                  _const_spec(seg.shape)],
        out_specs=[rows_spec, rows_spec,
                   pl.BlockSpec((tm // tk, ATTN_HEADS * ATTN_V_ROWS, tk), lambda i: (i, 0, 0))],
        out_shape=[qkv_shape, qkv_shape,
                   jax.ShapeDtypeStruct((m // tk, ATTN_HEADS * ATTN_V_ROWS, tk), BF16)],
        compiler_params=_params(("arbitrary",)), name="attn_in",
    )(x2, norm_g[None, :], wqk, wvt, cos_t, sin1_t, sin2_t, jnp.tile(q_norm_g, 2)[None, :],
      jnp.tile(k_norm_g, 2)[None, :], seg)

    vec64 = _const_spec((1, ATTN_HEAD_DIM))
    tile_spec = pl.BlockSpec((tq, D_MODEL), lambda b, qi: (b * qt + qi, 0))
    return pl.pallas_call(
        functools.partial(_flash_kernel, lambda_init=lambda_init),
        grid=(batch, qt),
        in_specs=[tile_spec,
                  pl.BlockSpec((seq, D_MODEL), lambda b, qi: (b, 0)),
                  pl.BlockSpec((seq // tk, ATTN_HEADS * ATTN_V_ROWS, tk), lambda b, qi: (b, 0, 0)),
                  tile_spec, vec64, vec64, vec64, vec64, _const_spec((1, ATTN_V_DIM)),
                  _const_spec((D_MODEL, D_MODEL))],
        out_specs=tile_spec,
        out_shape=jax.ShapeDtypeStruct((m, D_MODEL), F32),
        scratch_shapes=[pltpu.VMEM((ATTN_HEADS, ATTN_V_ROWS, 2 * tq), F32)],
        compiler_params=_params(("arbitrary", "arbitrary")), name="flash",
    )(q, k, vt, x2, lq1[None, :], lk1[None, :], lq2[None, :], lk2[None, :], subln_g[None, :],
      out_w.astype(BF16))


def _ffn_kernel(*refs, tiles_per_seq, n_casts):
    x_ref, g_ref, wup_ref, cw_ref, cb_ref, wd_ref = refs[:6]
    cast_in = refs[6:6 + n_casts]
    o_ref = refs[6 + n_casts]
    cast_out = refs[7 + n_casts:7 + 2 * n_casts]
    buf_g_ref, buf_u_ref, act_ref = refs[7 + 2 * n_casts:]
    first = pl.program_id(0) % tiles_per_seq == 0
    _conv_history_reset(buf_g_ref, first)
    _conv_history_reset(buf_u_ref, first)
    x = x_ref[...]
    h = _rmsnorm(x, g_ref[...]).astype(BF16)
    for c0 in range(0, D_FF, FFN_COLS):
        _conv_store(buf_g_ref, c0, _dot(h, wup_ref[:, c0:c0 + FFN_COLS]))
        _conv_store(buf_u_ref, c0, _dot(h, wup_ref[:, D_FF + c0:D_FF + c0 + FFN_COLS]))
        for l0 in range(c0, c0 + FFN_COLS, V7X_LANES):
            gl = slice(l0, l0 + V7X_LANES)
            ul = slice(D_FF + l0, D_FF + l0 + V7X_LANES)
            yg = _causal_conv(buf_g_ref, l0 // V7X_LANES, cw_ref[:, gl], cb_ref[:, gl])
            yu = _causal_conv(buf_u_ref, l0 // V7X_LANES, cw_ref[:, ul], cb_ref[:, ul])
            act_ref[:, gl] = (_silu(yg) * yu).astype(BF16)
    _conv_history_carry(buf_g_ref)
    _conv_history_carry(buf_u_ref)
    for src, dst in zip(cast_in, cast_out):
        dst[...] = src[...].astype(BF16)
    o_ref[...] = x + _dot(act_ref[...], wd_ref[...])


def _cast_blocking(rows, cols, n_steps):
    for ncb in (1, 2, 4, 8):
        nrb = n_steps // ncb
        if (rows % nrb == 0 and (rows // nrb) % V7X_BF16_SUBLANES == 0 and cols % ncb == 0
                and (ncb == 1 or (cols // ncb) % V7X_LANES == 0)):
            return rows // nrb, cols // ncb, ncb
    raise ValueError(f"no {n_steps}-block tiling for a ({rows}, {cols}) weight")


def _ffn_layer(x2, seq, norm_g, up_w, conv_w, conv_b, down_w, casts):
    m = x2.shape[0]
    tm = ROW_TILE
    steps = m // tm
    row = pl.BlockSpec((tm, D_MODEL), lambda i: (i, 0))
    cast_specs, cast_shapes = [], []
    for w, layer in casts:
        _, r, c = w.shape
        br, bc, ncb = _cast_blocking(r, c, steps)
        cast_specs.append((pl.BlockSpec((None, br, bc), lambda i, layer=layer, ncb=ncb: (layer, i // ncb, i % ncb)),
                           pl.BlockSpec((br, bc), lambda i, ncb=ncb: (i // ncb, i % ncb))))
        cast_shapes.append(jax.ShapeDtypeStruct((r, c), BF16))
    outs = pl.pallas_call(
        functools.partial(_ffn_kernel, tiles_per_seq=seq // tm, n_casts=len(casts)),
        grid=(steps,),
        in_specs=[row, _const_spec((1, D_MODEL)), _const_spec((D_MODEL, 2 * D_FF)), _const_spec((FFN_CONV, 2 * D_FF)),
                  _const_spec((1, 2 * D_FF)), _const_spec((D_FF, D_MODEL))] + [s[0] for s in cast_specs],
        out_specs=[row] + [s[1] for s in cast_specs],
        out_shape=[jax.ShapeDtypeStruct((m, D_MODEL), F32)] + cast_shapes,
        scratch_shapes=[pltpu.VMEM((D_FF // V7X_LANES, V7X_SUBLANES + tm, V7X_LANES), F32),
                        pltpu.VMEM((D_FF // V7X_LANES, V7X_SUBLANES + tm, V7X_LANES), F32),
                        pltpu.VMEM((tm, D_FF), BF16)],
        compiler_params=_params(("arbitrary",)), name="ffn",
    )(x2, norm_g[None, :], up_w, conv_w, conv_b[None, :], down_w, *[w for w, _ in casts])
    return outs[0], outs[1:]


def kernel(x, positions, norm_mix_g, norm_ffn_g, ssd_in_w, ssd_conv_w, ssd_conv_b, ssd_dt_bias, ssd_a_log, ssd_d, ssd_norm_g, ssd_out_w, attn_in_w, attn_q_norm_g, attn_k_norm_g, attn_lq1, attn_lk1, attn_lq2, attn_lk2, attn_subln_g, attn_out_w, ffn_up_w, ffn_conv_w, ffn_conv_b, ffn_down_w):
    batch, seq, d = x.shape
    depth = norm_mix_g.shape[0]
    assert d == D_MODEL and seq % ROW_TILE == 0 and ROW_TILE % ATTN_TILE == 0 and ATTN_TILE % CHUNK == 0
    x2 = x.reshape(batch * seq, d)
    tables = _rope_tables(positions)

    def mixer_weights(i):
        j = i // N_MIXERS
        return [(ssd_in_w, j), (ssd_out_w, j)] if i % N_MIXERS == 0 else [(attn_in_w, j), (attn_out_w, j)]

    mix_w = [w[j].astype(BF16) for w, j in mixer_weights(0)]
    ffn_w = [ffn_up_w[0].astype(BF16), ffn_down_w[0].astype(BF16)]
    for i in range(depth):
        j = i // N_MIXERS
        if i % N_MIXERS == 0:
            x2 = _ssd_layer(x2, batch, seq, norm_mix_g[i], mix_w[0], ssd_conv_w[j], ssd_conv_b[j],
                            ssd_dt_bias[j], ssd_a_log[j], ssd_d[j], ssd_norm_g[j], mix_w[1])
        else:
            lambda_init = 0.8 - 0.6 * math.exp(-0.3 * i)
            x2 = _attn_layer(x2, batch, seq, tables, norm_mix_g[i], mix_w[0], attn_q_norm_g[j],
                             attn_k_norm_g[j], attn_lq1[j], attn_lk1[j], attn_lq2[j], attn_lk2[j],
                             attn_subln_g[j], mix_w[1], lambda_init)
        casts = mixer_weights(i + 1) + [(ffn_up_w, i + 1), (ffn_down_w, i + 1)] if i + 1 < depth else []
        x2, cast = _ffn_layer(x2, seq, norm_ffn_g[i], ffn_w[0], ffn_conv_w[i], ffn_conv_b[i], ffn_w[1], casts)
        mix_w, ffn_w = cast[:2], cast[2:]
    return x2.reshape(batch, seq, d)
```

```python
import functools
import math

import jax
import jax.numpy as jnp
import numpy as np
from jax import lax
from jax.experimental import pallas as pl
from jax.experimental.pallas import tpu as pltpu

F32 = jnp.float32
BF16 = jnp.bfloat16

V7X_LANES = 128
V7X_SUBLANES = 8
V7X_BF16_SUBLANES = 16
V7X_VMEM_LIMIT_BYTES = 56 * 1024 * 1024

D_MODEL = 1024
CHUNK = 64
EPS = 1e-6
SSD_D_INNER = 2048
SSD_HEAD_DIM = 64
SSD_HEADS = 32
SSD_GROUPS = 4
SSD_HEADS_PER_GROUP = 8
SSD_STATE = 128
SSD_CONV = 4
SSD_CONV_CH = SSD_D_INNER + 2 * SSD_GROUPS * SSD_STATE
ATTN_HEADS = 8
ATTN_HEAD_DIM = 64
ATTN_V_DIM = 128
ATTN_V_ROWS = ATTN_V_DIM + V7X_BF16_SUBLANES
LOG2_E = math.log2(math.e)
ROPE_THETA = 500000.0
ROPE_DIM = 16
D_FF = 2816
FFN_CONV = 3
N_MIXERS = 2

ROW_TILE = 512
SSD_SCAN_CHUNK = 128
SSD_CONV_COLS = 512
SSD_OUT_ROWS = 256
FFN_COLS = 256
ATTN_TILE = 256


def _dot(a, b):
    return jnp.dot(a, b, preferred_element_type=F32)


def _dot_nt(a, b):
    return lax.dot_general(a, b, (((1,), (1,)), ((), ())), preferred_element_type=F32)


def _dot_tn(a, b):
    return lax.dot_general(a, b, (((0,), (0,)), ((), ())), preferred_element_type=F32)


def _silu(x):
    h = 0.5 * x
    return h + h * jnp.tanh(h)


def _rmsnorm(x, g):
    return x * lax.rsqrt(jnp.mean(x * x, axis=-1, keepdims=True) + EPS) * g


def _conv_store(buf_ref, c0, a):
    for j in range(a.shape[1] // V7X_LANES):
        buf_ref[c0 // V7X_LANES + j, V7X_SUBLANES:, :] = a[:, j * V7X_LANES:(j + 1) * V7X_LANES]


def _causal_conv(buf_ref, slab, w, b):
    kw = w.shape[0]
    tm = buf_ref.shape[1] - V7X_SUBLANES
    y = b + w[kw - 1:kw] * buf_ref[slab, V7X_SUBLANES:, :]
    for j in range(1, kw):
        y = y + w[kw - 1 - j:kw - j] * buf_ref[slab, V7X_SUBLANES - j:V7X_SUBLANES - j + tm, :]
    return y


def _conv_history_reset(buf_ref, first):
    @pl.when(first)
    def _():
        buf_ref[:, :V7X_SUBLANES, :] = jnp.zeros((buf_ref.shape[0], V7X_SUBLANES, V7X_LANES), buf_ref.dtype)


def _conv_history_carry(buf_ref):
    tm = buf_ref.shape[1] - V7X_SUBLANES
    buf_ref[:, :V7X_SUBLANES, :] = buf_ref[:, tm:, :]


def _const_spec(shape):
    nd = len(shape)
    return pl.BlockSpec(shape, lambda *_: (0,) * nd, pipeline_mode=pl.Buffered(1))


def _params(semantics):
    return pltpu.CompilerParams(dimension_semantics=semantics,
                                vmem_limit_bytes=V7X_VMEM_LIMIT_BYTES)


def _rope_kernel(pos_ref, freq_ref, m1_ref, m2_ref, c_ref, s1_ref, s2_ref):
    ang = pos_ref[...].astype(F32) * freq_ref[...]
    s = jnp.sin(ang)
    c_ref[...] = jnp.cos(ang)
    s1_ref[...] = s * m1_ref[...]
    s2_ref[...] = s * m2_ref[...]


def _rope_tables(positions):
    b, s = positions.shape
    m = b * s
    inv_freq = ROPE_THETA ** (-jnp.arange(0, ROPE_DIM, 2, dtype=F32) / ROPE_DIM)
    d = jnp.arange(V7X_LANES) % ATTN_HEAD_DIM
    half = ROPE_DIM // 2
    freq = jnp.where(d < ROPE_DIM, inv_freq[d % half], 0.0).astype(F32)[None, :]
    m1 = jnp.where(d < half, -1.0, 0.0).astype(F32)[None, :]
    m2 = jnp.where((d >= half) & (d < ROPE_DIM), 1.0, 0.0).astype(F32)[None, :]
    pos = jnp.broadcast_to(positions.reshape(m, 1), (m, V7X_LANES))
    row = pl.BlockSpec((s, V7X_LANES), lambda i: (i, 0))
    vec = pl.BlockSpec((1, V7X_LANES), lambda i: (0, 0))
    out = jax.ShapeDtypeStruct((m, V7X_LANES), F32)
    return pl.pallas_call(
        _rope_kernel, grid=(b,), in_specs=[row, vec, vec, vec], out_specs=[row, row, row],
        out_shape=[out, out, out], compiler_params=_params(("arbitrary",)), name="rope_tables",
    )(pos, freq, m1, m2)


def _ssd_in_kernel(x_ref, g_ref, w_ref, cw_ref, cb_ref, dtb_ref,
                   z_ref, xs_ref, b_ref, c_ref, dt_ref, buf_ref, *, tiles_per_seq):
    _conv_history_reset(buf_ref, pl.program_id(0) % tiles_per_seq == 0)
    h = _rmsnorm(x_ref[...], g_ref[...]).astype(BF16)
    gn = SSD_GROUPS * SSD_STATE
    n_chunks = SSD_CONV_CH // SSD_CONV_COLS
    zw = SSD_CONV_COLS
    xbc0 = SSD_D_INNER
    dt0 = SSD_D_INNER + SSD_CONV_CH

    def project_z(c):
        if (c + 1) * zw <= SSD_D_INNER:
            z_ref[:, c * zw:(c + 1) * zw] = _dot(h, w_ref[:, c * zw:(c + 1) * zw]).astype(BF16)
        elif c * zw == SSD_D_INNER:
            dt = _dot(h, w_ref[:, dt0:]) + dtb_ref[...]
            dt_ref[...] = jnp.zeros(dt_ref.shape, F32)
            dt_ref[:, :SSD_HEADS] = jnp.maximum(dt, 0.0) + jnp.log1p(jnp.exp(-jnp.abs(dt)))

    _conv_store(buf_ref, 0, _dot(h, w_ref[:, xbc0:xbc0 + SSD_CONV_COLS]))
    for c in range(n_chunks):
        c0 = c * SSD_CONV_COLS
        if c + 1 < n_chunks:
            _conv_store(buf_ref, c0 + SSD_CONV_COLS,
                        _dot(h, w_ref[:, xbc0 + c0 + SSD_CONV_COLS:xbc0 + c0 + 2 * SSD_CONV_COLS]))
        project_z(c)
        for l0 in range(c0, c0 + SSD_CONV_COLS, V7X_LANES):
            lanes = slice(l0, l0 + V7X_LANES)
            y = _silu(_causal_conv(buf_ref, l0 // V7X_LANES, cw_ref[:, lanes], cb_ref[:, lanes])).astype(BF16)
            if l0 < SSD_D_INNER:
                xs_ref[:, lanes] = y
            elif l0 < SSD_D_INNER + gn:
                b_ref[:, l0 - SSD_D_INNER:l0 - SSD_D_INNER + V7X_LANES] = y
            else:
                c_ref[:, l0 - SSD_D_INNER - gn:l0 - SSD_D_INNER - gn + V7X_LANES] = y
    _conv_history_carry(buf_ref)


def _ssd_core_kernel(xs_ref, b_ref, c_ref, dt_ref, z_ref, x_ref, alog_ref, dskip_ref, e_ref,
                     ng_ref, wo_ref, o_ref, h_ref, y_ref):
    ts = xs_ref.shape[0]
    L = SSD_SCAN_CHUNK
    P2 = 2 * SSD_HEAD_DIM
    GP = SSD_HEADS_PER_GROUP * SSD_HEAD_DIM

    @pl.when(pl.program_id(1) == 0)
    def _():
        h_ref[...] = jnp.zeros_like(h_ref)

    row = lax.broadcasted_iota(jnp.int32, (L, L), 0)
    col = lax.broadcasted_iota(jnp.int32, (L, L), 1)
    causal = row >= col
    tril = jnp.where(causal, 1.0, 0.0).astype(BF16)
    left = lax.broadcasted_iota(jnp.int32, (L, P2), 1) < SSD_HEAD_DIM
    a_row = -jnp.exp(alog_ref[...])

    def chunk_body(ci, carry):
        rows = pl.ds(pl.multiple_of(ci * L, L), L)
        dt = dt_ref[rows, :]
        da = dt * a_row
        hi = da.astype(BF16)
        r1 = da - hi.astype(F32)
        mid = r1.astype(BF16)
        lo = (r1 - mid.astype(F32)).astype(BF16)
        cs = _dot(tril, hi) + _dot(tril, mid) + _dot(tril, lo)
        cs_t = cs.T
        ecs = jnp.exp(cs)
        dec = jnp.exp(cs[L - 1:L, :] - cs)
        stacked = jnp.concatenate([dt, dec, ecs], axis=0).astype(BF16)
        wide = _dot(stacked, e_ref[...])
        dt_e, dec_e, ecs_e = wide[:L], wide[L:2 * L], wide[2 * L:]
        xs = xs_ref[rows, :].astype(F32)
        xdt = xs * dt_e
        xdt_b = xdt.astype(BF16)
        dx_b = (xdt * dec_e).astype(BF16)
        for g in range(SSD_GROUPS):
            gc = slice(g * SSD_STATE, (g + 1) * SSD_STATE)
            hc = slice(g * GP, (g + 1) * GP)
            bg = b_ref[rows, gc]
            cg = c_ref[rows, gc]
            cb = _dot_nt(cg, bg)
            hst = h_ref[g]
            y_off = _dot(cg, hst.astype(BF16)) * ecs_e[:, hc]
            h_ref[g] = hst * ecs_e[L - 1:L, hc] + _dot_tn(bg, dx_b[:, hc])
            for p in range(SSD_HEADS_PER_GROUP // 2):
                h0 = g * SSD_HEADS_PER_GROUP + 2 * p
                pc = slice(h0 * SSD_HEAD_DIM, h0 * SSD_HEAD_DIM + P2)
                pair = xdt_b[:, pc]

                def lmat(hd):
                    seg = cs[:, hd:hd + 1] - cs_t[hd:hd + 1, :]
                    return (cb * jnp.exp(jnp.where(causal, seg, -jnp.inf))).astype(BF16)

                lhs = jnp.concatenate([lmat(h0), lmat(h0 + 1)], axis=1)
                rhs = jnp.concatenate([jnp.where(left, pair, 0), jnp.where(left, 0, pair)], axis=0)
                y_ref[rows, pc] = (_dot(lhs, rhs) + y_off[:, p * P2:(p + 1) * P2]
                                   + xs[:, pc] * dskip_ref[:, pc])
        return carry

    lax.fori_loop(0, ts // L, chunk_body, 0, unroll=True)

    gw = SSD_D_INNER // SSD_GROUPS
    for r0 in range(0, ts, SSD_OUT_ROWS):
        rows = slice(r0, r0 + SSD_OUT_ROWS)
        gated = y_ref[rows, :] * _silu(z_ref[rows, :].astype(F32))
        parts = []
        for g in range(SSD_GROUPS):
            gg = gated[:, g * gw:(g + 1) * gw]
            parts.append(gg * lax.rsqrt(jnp.mean(gg * gg, axis=-1, keepdims=True) + EPS))
        yn = (jnp.concatenate(parts, axis=-1) * ng_ref[...]).astype(BF16)
        o_ref[rows, :] = x_ref[rows, :] + _dot(yn, wo_ref[...])


def _ssd_layer(x2, batch, seq, norm_g, in_w_layers, layer, conv_w, conv_b, dt_bias, a_log, d_skip, ssd_norm_g,
               out_w):
    m = x2.shape[0]
    tm = ROW_TILE
    gn = SSD_GROUPS * SSD_STATE
    alog = jnp.pad(a_log, (0, V7X_LANES - SSD_HEADS))[None, :]

    def rows(width):
        return pl.BlockSpec((tm, width), lambda i: (i, 0))

    z, xs, bm, cm, dt = pl.pallas_call(
        functools.partial(_ssd_in_kernel, tiles_per_seq=seq // tm),
        grid=(m // tm,),
        in_specs=[rows(D_MODEL), _const_spec((1, D_MODEL)),
                  pl.BlockSpec((None,) + in_w_layers.shape[1:], lambda i: (layer, 0, 0),
                               pipeline_mode=pl.Buffered(1)),
                  _const_spec(conv_w.shape), _const_spec((1, SSD_CONV_CH)), _const_spec((1, SSD_HEADS))],
        out_specs=[rows(SSD_D_INNER), rows(SSD_D_INNER), rows(gn), rows(gn), rows(V7X_LANES)],
        out_shape=[jax.ShapeDtypeStruct((m, SSD_D_INNER), BF16), jax.ShapeDtypeStruct((m, SSD_D_INNER), BF16),
                   jax.ShapeDtypeStruct((m, gn), BF16), jax.ShapeDtypeStruct((m, gn), BF16),
                   jax.ShapeDtypeStruct((m, V7X_LANES), F32)],
        scratch_shapes=[pltpu.VMEM((SSD_CONV_CH // V7X_LANES, V7X_SUBLANES + tm, V7X_LANES), F32)],
        compiler_params=_params(("arbitrary",)), name="ssd_in",
    )(x2, norm_g[None, :], in_w_layers, conv_w, conv_b[None, :], dt_bias[None, :])

    expand = jnp.asarray(np.arange(V7X_LANES)[:, None] == (np.arange(SSD_D_INNER) // SSD_HEAD_DIM)[None, :], BF16)
    dskip = jnp.repeat(d_skip, SSD_HEAD_DIM)[None, :]
    ts = ROW_TILE
    spt = seq // ts

    def tile(width):
        return pl.BlockSpec((ts, width), lambda b, s: (b * spt + s, 0))

    return pl.pallas_call(
        _ssd_core_kernel,
        grid=(batch, spt),
        in_specs=[tile(SSD_D_INNER), tile(gn), tile(gn), tile(V7X_LANES), tile(SSD_D_INNER), tile(D_MODEL),
                  _const_spec((1, V7X_LANES)), _const_spec((1, SSD_D_INNER)), _const_spec(expand.shape),
                  _const_spec((1, SSD_D_INNER)), _const_spec((SSD_D_INNER, D_MODEL))],
        out_specs=tile(D_MODEL),
        out_shape=jax.ShapeDtypeStruct((m, D_MODEL), F32),
        scratch_shapes=[pltpu.VMEM((SSD_GROUPS, SSD_STATE, SSD_HEADS_PER_GROUP * SSD_HEAD_DIM), F32),
                        pltpu.VMEM((ts, SSD_D_INNER), F32)],
        compiler_params=_params(("arbitrary", "arbitrary")), name="ssd_core",
    )(xs, bm, cm, dt, z, x2, alog, dskip, expand, ssd_norm_g[None, :], out_w.astype(BF16))


def _attn_in_kernel(x_ref, g_ref, wqk_ref, wvt_ref, c_ref, s1_ref, s2_ref, qg_ref, kg_ref, seg_ref,
                    q_ref, k_ref, vt_ref):
    t = ATTN_TILE
    hw = 2 * ATTN_HEAD_DIM
    half = ROPE_DIM // 2
    tm = x_ref.shape[0]
    h = _rmsnorm(x_ref[...], g_ref[...]).astype(BF16)
    cos_t, sin1_t, sin2_t = c_ref[...], s1_ref[...], s2_ref[...]
    q_scale = LOG2_E / math.sqrt(ATTN_HEAD_DIM)
    for c0, o_ref, gain_ref, scale in ((0, q_ref, qg_ref, q_scale), (D_MODEL, k_ref, kg_ref, None)):
        a = _dot(h, wqk_ref[:, c0:c0 + D_MODEL])
        for p0 in range(0, D_MODEL, 2 * hw):
            pair = a[:, p0:p0 + 2 * hw]
            ss = _dot((pair * pair).astype(BF16), seg_ref[...])
            inv = lax.rsqrt(ss * (1.0 / ATTN_HEAD_DIM) + EPS)
            for j in range(2):
                lanes = slice(j * hw, (j + 1) * hw)
                xn = pair[:, lanes] * inv[:, lanes] * gain_ref[...]
                y = (xn * cos_t + pltpu.roll(xn, hw - half, axis=1) * sin1_t
                     + pltpu.roll(xn, half, axis=1) * sin2_t)
                o_ref[:, p0 + j * hw:p0 + (j + 1) * hw] = (y if scale is None else y * scale).astype(BF16)
    vt = _dot_nt(wvt_ref[...], h).astype(BF16)
    ones = jnp.ones((ATTN_V_ROWS - ATTN_V_DIM, tm), BF16)
    pieces = []
    for g in range(ATTN_HEADS):
        pieces += [vt[g * ATTN_V_DIM:(g + 1) * ATTN_V_DIM], ones]
    vt = jnp.concatenate(pieces, axis=0)
    for j in range(tm // t):
        vt_ref[j] = vt[:, j * t:(j + 1) * t]


def _flash_kernel(q_ref, k_ref, vt_ref, x_ref, lq1_ref, lk1_ref, lq2_ref, lk2_ref, sg_ref, wo_ref,
                  o_ref, acc_ref, *, lambda_init):
    t = ATTN_TILE
    hw = 2 * ATTN_HEAD_DIM
    qi = pl.program_id(1)
    left_row = lax.broadcasted_iota(jnp.int32, (1, V7X_LANES), 1) < ATTN_HEAD_DIM
    heads = range(ATTN_HEADS)

    qa, qb = [], []
    for g in heads:
        qh = q_ref[:, g * hw:(g + 1) * hw]
        qa.append(jnp.where(left_row, qh, 0))
        qb.append(jnp.where(left_row, 0, qh))

    def kv_tile(kt, carry, masked=False):
        rows = pl.ds(pl.multiple_of(kt * t, t), t)
        scores = []
        for g in heads:
            ks = k_ref[rows, g * hw:(g + 1) * hw]
            scores.append((_dot_nt(ks, qa[g]), _dot_nt(ks, qb[g])))
        if masked:
            kc = lax.broadcasted_iota(jnp.int32, (t, t), 0) // CHUNK
            qc = lax.broadcasted_iota(jnp.int32, (t, t), 1) // CHUNK
            allowed = kc <= qc
            scores = [tuple(jnp.where(allowed, s, -jnp.inf) for s in pair) for pair in scores]
        probs, out = [], []
        for g in heads:
            new, alphas, ps = [], [], []
            for c, s in enumerate(scores[g]):
                m = jnp.max(s, axis=0, keepdims=True)
                if carry is not None:
                    m_old = carry[g][c]
                    m = jnp.maximum(m_old, m)
                    alphas.append(jnp.exp2(m_old - m))
                ps.append(jnp.exp2(s - m))
                new.append(m)
            probs.append((jnp.concatenate(ps, axis=1).astype(BF16), alphas))
            out.append(tuple(new))
        for g in heads:
            p, alphas = probs[g]
            pv = _dot(vt_ref[kt, g * ATTN_V_ROWS:(g + 1) * ATTN_V_ROWS, :], p)
            if carry is None:
                acc_ref[g] = pv
            else:
                acc_ref[g] = acc_ref[g] * jnp.concatenate(alphas, axis=1) + pv
        return tuple(out)

    lax.fori_loop(0, qi, kv_tile, kv_tile(qi, None, True))

    lam = (jnp.exp(jnp.sum(lq1_ref[...] * lk1_ref[...], axis=-1, keepdims=True))
           - jnp.exp(jnp.sum(lq2_ref[...] * lk2_ref[...], axis=-1, keepdims=True)) + lambda_init)
    outs = []
    for g in heads:
        acc = acc_ref[g]
        la = acc[ATTN_V_DIM:ATTN_V_DIM + 1, :t]
        lb = acc[ATTN_V_DIM:ATTN_V_DIM + 1, t:]
        ot = acc[:ATTN_V_DIM, :t] / la - lam * (acc[:ATTN_V_DIM, t:] / lb)
        ot = ot * lax.rsqrt(jnp.mean(ot * ot, axis=0, keepdims=True) + EPS)
        outs.append((ot.T * sg_ref[...] * (1.0 - lambda_init)).astype(BF16))
    o_ref[...] = x_ref[...] + _dot(jnp.concatenate(outs, axis=1), wo_ref[...])


def _attn_layer(x2, batch, seq, tables, norm_g, in_w, q_norm_g, k_norm_g, lq1, lk1, lq2, lk2,
                subln_g, out_w, lambda_init):
    m = x2.shape[0]
    tm = ROW_TILE
    tk = tq = ATTN_TILE
    qt = seq // tq
    hw = 2 * ATTN_HEAD_DIM
    wqk = in_w.astype(BF16)
    wvt = in_w[:, 2 * D_MODEL:].T.astype(BF16)
    cos_t, sin1_t, sin2_t = tables
    lane = np.arange(2 * hw)
    seg = jnp.asarray(lane[:, None] // ATTN_HEAD_DIM == lane[None, :] // ATTN_HEAD_DIM, BF16)
    rows_spec = pl.BlockSpec((tm, D_MODEL), lambda i: (i, 0))
    table_spec = pl.BlockSpec((tm, V7X_LANES), lambda i: (i, 0))
    qkv_shape = jax.ShapeDtypeStruct((m, D_MODEL), BF16)
    q, k, vt = pl.pallas_call(
        _attn_in_kernel,
        grid=(m // tm,),
        in_specs=[rows_spec, _const_spec((1, D_MODEL)), _const_spec(wqk.shape), _const_spec(wvt.shape),
                  table_spec, table_spec, table_spec, _const_spec((1, hw)), _const_spec((1, hw)),
                  _const_spec(seg.shape)],
        out_specs=[rows_spec, rows_spec,
                   pl.BlockSpec((tm // tk, ATTN_HEADS * ATTN_V_ROWS, tk), lambda i: (i, 0, 0))],
        out_shape=[qkv_shape, qkv_shape,
                   jax.ShapeDtypeStruct((m // tk, ATTN_HEADS * ATTN_V_ROWS, tk), BF16)],
        compiler_params=_params(("arbitrary",)), name="attn_in",
    )(x2, norm_g[None, :], wqk, wvt, cos_t, sin1_t, sin2_t, jnp.tile(q_norm_g, 2)[None, :],
      jnp.tile(k_norm_g, 2)[None, :], seg)

    vec64 = _const_spec((1, ATTN_HEAD_DIM))
    tile_spec = pl.BlockSpec((tq, D_MODEL), lambda b, qi: (b * qt + qi, 0))
    return pl.pallas_call(
        functools.partial(_flash_kernel, lambda_init=lambda_init),
        grid=(batch, qt),
        in_specs=[tile_spec,
                  pl.BlockSpec((seq, D_MODEL), lambda b, qi: (b, 0)),
                  pl.BlockSpec((seq // tk, ATTN_HEADS * ATTN_V_ROWS, tk), lambda b, qi: (b, 0, 0)),
                  tile_spec, vec64, vec64, vec64, vec64, _const_spec((1, ATTN_V_DIM)),
                  _const_spec((D_MODEL, D_MODEL))],
        out_specs=tile_spec,
        out_shape=jax.ShapeDtypeStruct((m, D_MODEL), F32),
        scratch_shapes=[pltpu.VMEM((ATTN_HEADS, ATTN_V_ROWS, 2 * tq), F32)],
        compiler_params=_params(("arbitrary", "arbitrary")), name="flash",
    )(q, k, vt, x2, lq1[None, :], lk1[None, :], lq2[None, :], lk2[None, :], subln_g[None, :],
      out_w.astype(BF16))


def _ffn_kernel(*refs, tiles_per_seq, n_casts):
    x_ref, g_ref, wup_ref, cw_ref, cb_ref, wd_ref = refs[:6]
    cast_in = refs[6:6 + n_casts]
    o_ref = refs[6 + n_casts]
    cast_out = refs[7 + n_casts:7 + 2 * n_casts]
    buf_g_ref, buf_u_ref, act_ref = refs[7 + 2 * n_casts:]
    first = pl.program_id(0) % tiles_per_seq == 0
    _conv_history_reset(buf_g_ref, first)
    _conv_history_reset(buf_u_ref, first)
    x = x_ref[...]
    h = _rmsnorm(x, g_ref[...]).astype(BF16)
    for c0 in range(0, D_FF, FFN_COLS):
        _conv_store(buf_g_ref, c0, _dot(h, wup_ref[:, c0:c0 + FFN_COLS]))
        _conv_store(buf_u_ref, c0, _dot(h, wup_ref[:, D_FF + c0:D_FF + c0 + FFN_COLS]))
        for l0 in range(c0, c0 + FFN_COLS, V7X_LANES):
            gl = slice(l0, l0 + V7X_LANES)
            ul = slice(D_FF + l0, D_FF + l0 + V7X_LANES)
            yg = _causal_conv(buf_g_ref, l0 // V7X_LANES, cw_ref[:, gl], cb_ref[:, gl])
            yu = _causal_conv(buf_u_ref, l0 // V7X_LANES, cw_ref[:, ul], cb_ref[:, ul])
            act_ref[:, gl] = (_silu(yg) * yu).astype(BF16)
    _conv_history_carry(buf_g_ref)
    _conv_history_carry(buf_u_ref)
    for src, dst in zip(cast_in, cast_out):
        dst[...] = src[...].astype(BF16)
    o_ref[...] = x + _dot(act_ref[...], wd_ref[...])


def _cast_blocking(rows, cols, n_steps):
    for ncb in (1, 2, 4, 8):
        nrb = n_steps // ncb
        if (rows % nrb == 0 and (rows // nrb) % V7X_BF16_SUBLANES == 0 and cols % ncb == 0
                and (ncb == 1 or (cols // ncb) % V7X_LANES == 0)):
            return rows // nrb, cols // ncb, ncb
    raise ValueError(f"no {n_steps}-block tiling for a ({rows}, {cols}) weight")


def _ffn_layer(x2, seq, norm_g, up_w, conv_w, conv_b, down_w, casts):
    m = x2.shape[0]
    tm = ROW_TILE
    steps = m // tm
    row = pl.BlockSpec((tm, D_MODEL), lambda i: (i, 0))
    cast_specs, cast_shapes = [], []
    for w, layer in casts:
        _, r, c = w.shape
        br, bc, ncb = _cast_blocking(r, c, steps)
        cast_specs.append((pl.BlockSpec((None, br, bc), lambda i, layer=layer, ncb=ncb: (layer, i // ncb, i % ncb)),
                           pl.BlockSpec((br, bc), lambda i, ncb=ncb: (i // ncb, i % ncb))))
        cast_shapes.append(jax.ShapeDtypeStruct((r, c), BF16))
    outs = pl.pallas_call(
        functools.partial(_ffn_kernel, tiles_per_seq=seq // tm, n_casts=len(casts)),
        grid=(steps,),
        in_specs=[row, _const_spec((1, D_MODEL)), _const_spec((D_MODEL, 2 * D_FF)), _const_spec((FFN_CONV, 2 * D_FF)),
                  _const_spec((1, 2 * D_FF)), _const_spec((D_FF, D_MODEL))] + [s[0] for s in cast_specs],
        out_specs=[row] + [s[1] for s in cast_specs],
        out_shape=[jax.ShapeDtypeStruct((m, D_MODEL), F32)] + cast_shapes,
        scratch_shapes=[pltpu.VMEM((D_FF // V7X_LANES, V7X_SUBLANES + tm, V7X_LANES), F32),
                        pltpu.VMEM((D_FF // V7X_LANES, V7X_SUBLANES + tm, V7X_LANES), F32),
                        pltpu.VMEM((tm, D_FF), BF16)],
        compiler_params=_params(("arbitrary",)), name="ffn",
    )(x2, norm_g[None, :], up_w, conv_w, conv_b[None, :], down_w, *[w for w, _ in casts])
    return outs[0], outs[1:]


def kernel(x, positions, norm_mix_g, norm_ffn_g, ssd_in_w, ssd_conv_w, ssd_conv_b, ssd_dt_bias, ssd_a_log, ssd_d, ssd_norm_g, ssd_out_w, attn_in_w, attn_q_norm_g, attn_k_norm_g, attn_lq1, attn_lk1, attn_lq2, attn_lk2, attn_subln_g, attn_out_w, ffn_up_w, ffn_conv_w, ffn_conv_b, ffn_down_w):
    batch, seq, d = x.shape
    depth = norm_mix_g.shape[0]
    assert d == D_MODEL and seq % ROW_TILE == 0 and ROW_TILE % ATTN_TILE == 0 and ATTN_TILE % CHUNK == 0
    x2 = x.reshape(batch * seq, d)
    tables = _rope_tables(positions)

    def mixer_weights(i):
        j = i // N_MIXERS
        return [(ssd_out_w, j)] if i % N_MIXERS == 0 else [(attn_in_w, j), (attn_out_w, j)]

    ssd_in_bf = ssd_in_w.astype(BF16)
    mix_w = [w[j].astype(BF16) for w, j in mixer_weights(0)]
    ffn_w = [ffn_up_w[0].astype(BF16), ffn_down_w[0].astype(BF16)]
    for i in range(depth):
        j = i // N_MIXERS
        if i % N_MIXERS == 0:
            x2 = _ssd_layer(x2, batch, seq, norm_mix_g[i], ssd_in_bf, j, ssd_conv_w[j], ssd_conv_b[j],
                            ssd_dt_bias[j], ssd_a_log[j], ssd_d[j], ssd_norm_g[j], mix_w[0])
        else:
            lambda_init = 0.8 - 0.6 * math.exp(-0.3 * i)
            x2 = _attn_layer(x2, batch, seq, tables, norm_mix_g[i], mix_w[0], attn_q_norm_g[j],
                             attn_k_norm_g[j], attn_lq1[j], attn_lk1[j], attn_lq2[j], attn_lk2[j],
                             attn_subln_g[j], mix_w[1], lambda_init)
        casts = mixer_weights(i + 1) + [(ffn_up_w, i + 1), (ffn_down_w, i + 1)] if i + 1 < depth else []
        x2, cast = _ffn_layer(x2, seq, norm_ffn_g[i], ffn_w[0], ffn_conv_w[i], ffn_conv_b[i], ffn_w[1], casts)
        mix_w, ffn_w = cast[:-2], cast[-2:]
    return x2.reshape(batch, seq, d)
```

```python
import functools
import math

import jax
import jax.numpy as jnp
import numpy as np
from jax import lax
from jax.experimental import pallas as pl
from jax.experimental.pallas import tpu as pltpu

F32 = jnp.float32
BF16 = jnp.bfloat16

V7X_LANES = 128
V7X_SUBLANES = 8
V7X_BF16_SUBLANES = 16
V7X_VMEM_LIMIT_BYTES = 56 * 1024 * 1024

D_MODEL = 1024
CHUNK = 64
EPS = 1e-6
SSD_D_INNER = 2048
SSD_HEAD_DIM = 64
SSD_HEADS = 32
SSD_GROUPS = 4
SSD_HEADS_PER_GROUP = 8
SSD_STATE = 128
SSD_CONV = 4
SSD_CONV_CH = SSD_D_INNER + 2 * SSD_GROUPS * SSD_STATE
ATTN_HEADS = 8
ATTN_HEAD_DIM = 64
ATTN_V_DIM = 128
ATTN_V_ROWS = ATTN_V_DIM + V7X_BF16_SUBLANES
LOG2_E = math.log2(math.e)
ROPE_THETA = 500000.0
ROPE_DIM = 16
D_FF = 2816
FFN_CONV = 3
N_MIXERS = 2

ROW_TILE = 512
SSD_SCAN_CHUNK = 128
SSD_CONV_COLS = 512
SSD_OUT_ROWS = 256
FFN_COLS = 256
ATTN_TILE = 256


def _dot(a, b):
    return jnp.dot(a, b, preferred_element_type=F32)


def _dot_nt(a, b):
    return lax.dot_general(a, b, (((1,), (1,)), ((), ())), preferred_element_type=F32)


def _dot_tn(a, b):
    return lax.dot_general(a, b, (((0,), (0,)), ((), ())), preferred_element_type=F32)


def _silu(x):
    h = 0.5 * x
    return h + h * jnp.tanh(h)


def _rmsnorm(x, g):
    return x * lax.rsqrt(jnp.mean(x * x, axis=-1, keepdims=True) + EPS) * g


def _conv_store(buf_ref, c0, a):
    for j in range(a.shape[1] // V7X_LANES):
        buf_ref[c0 // V7X_LANES + j, V7X_SUBLANES:, :] = a[:, j * V7X_LANES:(j + 1) * V7X_LANES]


def _causal_conv(buf_ref, slab, w, b):
    kw = w.shape[0]
    tm = buf_ref.shape[1] - V7X_SUBLANES
    y = b + w[kw - 1:kw] * buf_ref[slab, V7X_SUBLANES:, :]
    for j in range(1, kw):
        y = y + w[kw - 1 - j:kw - j] * buf_ref[slab, V7X_SUBLANES - j:V7X_SUBLANES - j + tm, :]
    return y


def _conv_history_reset(buf_ref, first):
    @pl.when(first)
    def _():
        buf_ref[:, :V7X_SUBLANES, :] = jnp.zeros((buf_ref.shape[0], V7X_SUBLANES, V7X_LANES), buf_ref.dtype)


def _conv_history_carry(buf_ref):
    tm = buf_ref.shape[1] - V7X_SUBLANES
    buf_ref[:, :V7X_SUBLANES, :] = buf_ref[:, tm:, :]


def _const_spec(shape):
    nd = len(shape)
    return pl.BlockSpec(shape, lambda *_: (0,) * nd, pipeline_mode=pl.Buffered(1))


def _params(semantics):
    return pltpu.CompilerParams(dimension_semantics=semantics,
                                vmem_limit_bytes=V7X_VMEM_LIMIT_BYTES)


def _rope_kernel(pos_ref, freq_ref, c_ref, s_ref):
    ang = pos_ref[...].astype(F32) * freq_ref[...]
    c_ref[...] = jnp.cos(ang)
    s_ref[...] = jnp.sin(ang)


def _rope_tables(positions):
    b, s = positions.shape
    m = b * s
    half = ROPE_DIM // 2
    per_row = V7X_LANES // half
    inv_freq = ROPE_THETA ** (-jnp.arange(0, ROPE_DIM, 2, dtype=F32) / ROPE_DIM)
    pos = jnp.repeat(positions.reshape(m // per_row, per_row), half, axis=1)
    freq = jnp.tile(inv_freq, per_row)[None, :]
    packed = jax.ShapeDtypeStruct((m // per_row, V7X_LANES), F32)
    cos_p, sin_p = pl.pallas_call(_rope_kernel, out_shape=[packed, packed], name="rope_tables")(pos, freq)
    cos8 = cos_p.reshape(m, half)
    sin8 = sin_p.reshape(m, half)
    rest = ATTN_HEAD_DIM - ROPE_DIM

    def both_components(parts):
        comp = jnp.concatenate(parts, axis=1)
        return jnp.concatenate([comp, comp], axis=1)

    zeros = jnp.zeros((m, half), F32)
    cos_t = both_components([cos8, cos8, jnp.ones((m, rest), F32)])
    sin1_t = both_components([-sin8, zeros, jnp.zeros((m, rest), F32)])
    sin2_t = both_components([zeros, sin8, jnp.zeros((m, rest), F32)])
    return cos_t, sin1_t, sin2_t


def _ssd_in_kernel(x_ref, g_ref, w_ref, cw_ref, cb_ref, dtb_ref,
                   z_ref, xs_ref, b_ref, c_ref, dt_ref, buf_ref, *, tiles_per_seq):
    _conv_history_reset(buf_ref, pl.program_id(0) % tiles_per_seq == 0)
    h = _rmsnorm(x_ref[...], g_ref[...]).astype(BF16)
    gn = SSD_GROUPS * SSD_STATE
    n_chunks = SSD_CONV_CH // SSD_CONV_COLS
    zw = SSD_CONV_COLS
    xbc0 = SSD_D_INNER
    dt0 = SSD_D_INNER + SSD_CONV_CH

    def project_z(c):
        if (c + 1) * zw <= SSD_D_INNER:
            z_ref[:, c * zw:(c + 1) * zw] = _dot(h, w_ref[:, c * zw:(c + 1) * zw]).astype(BF16)
        elif c * zw == SSD_D_INNER:
            dt = _dot(h, w_ref[:, dt0:]) + dtb_ref[...]
            dt_ref[...] = jnp.zeros(dt_ref.shape, F32)
            dt_ref[:, :SSD_HEADS] = jnp.maximum(dt, 0.0) + jnp.log1p(jnp.exp(-jnp.abs(dt)))

    _conv_store(buf_ref, 0, _dot(h, w_ref[:, xbc0:xbc0 + SSD_CONV_COLS]))
    for c in range(n_chunks):
        c0 = c * SSD_CONV_COLS
        if c + 1 < n_chunks:
            _conv_store(buf_ref, c0 + SSD_CONV_COLS,
                        _dot(h, w_ref[:, xbc0 + c0 + SSD_CONV_COLS:xbc0 + c0 + 2 * SSD_CONV_COLS]))
        project_z(c)
        for l0 in range(c0, c0 + SSD_CONV_COLS, V7X_LANES):
            lanes = slice(l0, l0 + V7X_LANES)
            y = _silu(_causal_conv(buf_ref, l0 // V7X_LANES, cw_ref[:, lanes], cb_ref[:, lanes])).astype(BF16)
            if l0 < SSD_D_INNER:
                xs_ref[:, lanes] = y
            elif l0 < SSD_D_INNER + gn:
                b_ref[:, l0 - SSD_D_INNER:l0 - SSD_D_INNER + V7X_LANES] = y
            else:
                c_ref[:, l0 - SSD_D_INNER - gn:l0 - SSD_D_INNER - gn + V7X_LANES] = y
    _conv_history_carry(buf_ref)


def _ssd_core_kernel(xs_ref, b_ref, c_ref, dt_ref, z_ref, x_ref, alog_ref, dskip_ref, e_ref,
                     ng_ref, wo_ref, o_ref, h_ref, y_ref):
    ts = xs_ref.shape[0]
    L = SSD_SCAN_CHUNK
    P2 = 2 * SSD_HEAD_DIM
    GP = SSD_HEADS_PER_GROUP * SSD_HEAD_DIM

    @pl.when(pl.program_id(1) == 0)
    def _():
        h_ref[...] = jnp.zeros_like(h_ref)

    row = lax.broadcasted_iota(jnp.int32, (L, L), 0)
    col = lax.broadcasted_iota(jnp.int32, (L, L), 1)
    causal = row >= col
    tril = jnp.where(causal, 1.0, 0.0).astype(BF16)
    left = lax.broadcasted_iota(jnp.int32, (L, P2), 1) < SSD_HEAD_DIM
    a_row = -jnp.exp(alog_ref[...])

    def chunk_body(ci, carry):
        rows = pl.ds(pl.multiple_of(ci * L, L), L)
        dt = dt_ref[rows, :]
        da = dt * a_row
        hi = da.astype(BF16)
        r1 = da - hi.astype(F32)
        mid = r1.astype(BF16)
        lo = (r1 - mid.astype(F32)).astype(BF16)
        cs = _dot(tril, hi) + _dot(tril, mid) + _dot(tril, lo)
        cs_t = cs.T
        ecs = jnp.exp(cs)
        dec = jnp.exp(cs[L - 1:L, :] - cs)
        stacked = jnp.concatenate([dt, dec, ecs], axis=0).astype(BF16)
        wide = _dot(stacked, e_ref[...])
        dt_e, dec_e, ecs_e = wide[:L], wide[L:2 * L], wide[2 * L:]
        xs = xs_ref[rows, :].astype(F32)
        xdt = xs * dt_e
        xdt_b = xdt.astype(BF16)
        dx_b = (xdt * dec_e).astype(BF16)
        for g in range(SSD_GROUPS):
            gc = slice(g * SSD_STATE, (g + 1) * SSD_STATE)
            hc = slice(g * GP, (g + 1) * GP)
            bg = b_ref[rows, gc]
            cg = c_ref[rows, gc]
            cb = _dot_nt(cg, bg)
            hst = h_ref[g]
            y_off = _dot(cg, hst.astype(BF16)) * ecs_e[:, hc]
            h_ref[g] = hst * ecs_e[L - 1:L, hc] + _dot_tn(bg, dx_b[:, hc])
            for p in range(SSD_HEADS_PER_GROUP // 2):
                h0 = g * SSD_HEADS_PER_GROUP + 2 * p
                pc = slice(h0 * SSD_HEAD_DIM, h0 * SSD_HEAD_DIM + P2)
                pair = xdt_b[:, pc]

                def lmat(hd):
                    seg = cs[:, hd:hd + 1] - cs_t[hd:hd + 1, :]
                    return (cb * jnp.exp(jnp.where(causal, seg, -jnp.inf))).astype(BF16)

                lhs = jnp.concatenate([lmat(h0), lmat(h0 + 1)], axis=1)
                rhs = jnp.concatenate([jnp.where(left, pair, 0), jnp.where(left, 0, pair)], axis=0)
                y_ref[rows, pc] = (_dot(lhs, rhs) + y_off[:, p * P2:(p + 1) * P2]
                                   + xs[:, pc] * dskip_ref[:, pc])
        return carry

    lax.fori_loop(0, ts // L, chunk_body, 0, unroll=True)

    gw = SSD_D_INNER // SSD_GROUPS
    for r0 in range(0, ts, SSD_OUT_ROWS):
        rows = slice(r0, r0 + SSD_OUT_ROWS)
        gated = y_ref[rows, :] * _silu(z_ref[rows, :].astype(F32))
        parts = []
        for g in range(SSD_GROUPS):
            gg = gated[:, g * gw:(g + 1) * gw]
            parts.append(gg * lax.rsqrt(jnp.mean(gg * gg, axis=-1, keepdims=True) + EPS))
        yn = (jnp.concatenate(parts, axis=-1) * ng_ref[...]).astype(BF16)
        o_ref[rows, :] = x_ref[rows, :] + _dot(yn, wo_ref[...])


def _ssd_layer(x2, batch, seq, norm_g, in_w_layers, layer, conv_w, conv_b, dt_bias, a_log, d_skip, ssd_norm_g,
               out_w):
    m = x2.shape[0]
    tm = ROW_TILE
    gn = SSD_GROUPS * SSD_STATE
    alog = jnp.pad(a_log, (0, V7X_LANES - SSD_HEADS))[None, :]

    def rows(width):
        return pl.BlockSpec((tm, width), lambda i: (i, 0))

    z, xs, bm, cm, dt = pl.pallas_call(
        functools.partial(_ssd_in_kernel, tiles_per_seq=seq // tm),
        grid=(m // tm,),
        in_specs=[rows(D_MODEL), _const_spec((1, D_MODEL)),
                  pl.BlockSpec((None,) + in_w_layers.shape[1:], lambda i: (layer, 0, 0),
                               pipeline_mode=pl.Buffered(1)),
                  _const_spec(conv_w.shape), _const_spec((1, SSD_CONV_CH)), _const_spec((1, SSD_HEADS))],
        out_specs=[rows(SSD_D_INNER), rows(SSD_D_INNER), rows(gn), rows(gn), rows(V7X_LANES)],
        out_shape=[jax.ShapeDtypeStruct((m, SSD_D_INNER), BF16), jax.ShapeDtypeStruct((m, SSD_D_INNER), BF16),
                   jax.ShapeDtypeStruct((m, gn), BF16), jax.ShapeDtypeStruct((m, gn), BF16),
                   jax.ShapeDtypeStruct((m, V7X_LANES), F32)],
        scratch_shapes=[pltpu.VMEM((SSD_CONV_CH // V7X_LANES, V7X_SUBLANES + tm, V7X_LANES), F32)],
        compiler_params=_params(("arbitrary",)), name="ssd_in",
    )(x2, norm_g[None, :], in_w_layers, conv_w, conv_b[None, :], dt_bias[None, :])

    expand = jnp.asarray(np.arange(V7X_LANES)[:, None] == (np.arange(SSD_D_INNER) // SSD_HEAD_DIM)[None, :], BF16)
    dskip = jnp.repeat(d_skip, SSD_HEAD_DIM)[None, :]
    ts = ROW_TILE
    spt = seq // ts

    def tile(width):
        return pl.BlockSpec((ts, width), lambda b, s: (b * spt + s, 0))

    return pl.pallas_call(
        _ssd_core_kernel,
        grid=(batch, spt),
        in_specs=[tile(SSD_D_INNER), tile(gn), tile(gn), tile(V7X_LANES), tile(SSD_D_INNER), tile(D_MODEL),
                  _const_spec((1, V7X_LANES)), _const_spec((1, SSD_D_INNER)), _const_spec(expand.shape),
                  _const_spec((1, SSD_D_INNER)), _const_spec((SSD_D_INNER, D_MODEL))],
        out_specs=tile(D_MODEL),
        out_shape=jax.ShapeDtypeStruct((m, D_MODEL), F32),
        scratch_shapes=[pltpu.VMEM((SSD_GROUPS, SSD_STATE, SSD_HEADS_PER_GROUP * SSD_HEAD_DIM), F32),
                        pltpu.VMEM((ts, SSD_D_INNER), F32)],
        compiler_params=_params(("arbitrary", "arbitrary")), name="ssd_core",
    )(xs, bm, cm, dt, z, x2, alog, dskip, expand, ssd_norm_g[None, :], out_w.astype(BF16))


def _attn_in_kernel(x_ref, g_ref, wqk_ref, wvt_ref, c_ref, s1_ref, s2_ref, qg_ref, kg_ref, seg_ref,
                    q_ref, k_ref, vt_ref):
    t = ATTN_TILE
    hw = 2 * ATTN_HEAD_DIM
    half = ROPE_DIM // 2
    tm = x_ref.shape[0]
    h = _rmsnorm(x_ref[...], g_ref[...]).astype(BF16)
    cos_t, sin1_t, sin2_t = c_ref[...], s1_ref[...], s2_ref[...]
    q_scale = LOG2_E / math.sqrt(ATTN_HEAD_DIM)
    for c0, o_ref, gain_ref, scale in ((0, q_ref, qg_ref, q_scale), (D_MODEL, k_ref, kg_ref, None)):
        a = _dot(h, wqk_ref[:, c0:c0 + D_MODEL])
        for p0 in range(0, D_MODEL, 2 * hw):
            pair = a[:, p0:p0 + 2 * hw]
            ss = _dot((pair * pair).astype(BF16), seg_ref[...])
            inv = lax.rsqrt(ss * (1.0 / ATTN_HEAD_DIM) + EPS)
            for j in range(2):
                lanes = slice(j * hw, (j + 1) * hw)
                xn = pair[:, lanes] * inv[:, lanes] * gain_ref[...]
                y = (xn * cos_t + pltpu.roll(xn, hw - half, axis=1) * sin1_t
                     + pltpu.roll(xn, half, axis=1) * sin2_t)
                o_ref[:, p0 + j * hw:p0 + (j + 1) * hw] = (y if scale is None else y * scale).astype(BF16)
    vt = _dot_nt(wvt_ref[...], h).astype(BF16)
    ones = jnp.ones((ATTN_V_ROWS - ATTN_V_DIM, tm), BF16)
    pieces = []
    for g in range(ATTN_HEADS):
        pieces += [vt[g * ATTN_V_DIM:(g + 1) * ATTN_V_DIM], ones]
    vt = jnp.concatenate(pieces, axis=0)
    for j in range(tm // t):
        vt_ref[j] = vt[:, j * t:(j + 1) * t]


def _flash_kernel(q_ref, k_ref, vt_ref, x_ref, lq1_ref, lk1_ref, lq2_ref, lk2_ref, sg_ref, wo_ref,
                  o_ref, acc_ref, *, lambda_init):
    t = ATTN_TILE
    hw = 2 * ATTN_HEAD_DIM
    qi = pl.program_id(1)
    left_row = lax.broadcasted_iota(jnp.int32, (1, V7X_LANES), 1) < ATTN_HEAD_DIM
    heads = range(ATTN_HEADS)

    qa, qb = [], []
    for g in heads:
        qh = q_ref[:, g * hw:(g + 1) * hw]
        qa.append(jnp.where(left_row, qh, 0))
        qb.append(jnp.where(left_row, 0, qh))

    def kv_tile(kt, carry, masked=False):
        rows = pl.ds(pl.multiple_of(kt * t, t), t)
        scores = []
        for g in heads:
            ks = k_ref[rows, g * hw:(g + 1) * hw]
            scores.append((_dot_nt(ks, qa[g]), _dot_nt(ks, qb[g])))
        if masked:
            kc = lax.broadcasted_iota(jnp.int32, (t, t), 0) // CHUNK
            qc = lax.broadcasted_iota(jnp.int32, (t, t), 1) // CHUNK
            allowed = kc <= qc
            scores = [tuple(jnp.where(allowed, s, -jnp.inf) for s in pair) for pair in scores]
        probs, out = [], []
        for g in heads:
            new, alphas, ps = [], [], []
            for c, s in enumerate(scores[g]):
                m = jnp.max(s, axis=0, keepdims=True)
                if carry is not None:
                    m_old = carry[g][c]
                    m = jnp.maximum(m_old, m)
                    alphas.append(jnp.exp2(m_old - m))
                ps.append(jnp.exp2(s - m))
                new.append(m)
            probs.append((jnp.concatenate(ps, axis=1).astype(BF16), alphas))
            out.append(tuple(new))
        for g in heads:
            p, alphas = probs[g]
            pv = _dot(vt_ref[kt, g * ATTN_V_ROWS:(g + 1) * ATTN_V_ROWS, :], p)
            if carry is None:
                acc_ref[g] = pv
            else:
                acc_ref[g] = acc_ref[g] * jnp.concatenate(alphas, axis=1) + pv
        return tuple(out)

    lax.fori_loop(0, qi, kv_tile, kv_tile(qi, None, True))

    lam = (jnp.exp(jnp.sum(lq1_ref[...] * lk1_ref[...], axis=-1, keepdims=True))
           - jnp.exp(jnp.sum(lq2_ref[...] * lk2_ref[...], axis=-1, keepdims=True)) + lambda_init)
    outs = []
    for g in heads:
        acc = acc_ref[g]
        la = acc[ATTN_V_DIM:ATTN_V_DIM + 1, :t]
        lb = acc[ATTN_V_DIM:ATTN_V_DIM + 1, t:]
        ot = acc[:ATTN_V_DIM, :t] / la - lam * (acc[:ATTN_V_DIM, t:] / lb)
        ot = ot * lax.rsqrt(jnp.mean(ot * ot, axis=0, keepdims=True) + EPS)
        outs.append((ot.T * sg_ref[...] * (1.0 - lambda_init)).astype(BF16))
    o_ref[...] = x_ref[...] + _dot(jnp.concatenate(outs, axis=1), wo_ref[...])


def _attn_layer(x2, batch, seq, tables, norm_g, in_w, q_norm_g, k_norm_g, lq1, lk1, lq2, lk2,
                subln_g, out_w, lambda_init):
    m = x2.shape[0]
    tm = ROW_TILE
    tk = tq = ATTN_TILE
    qt = seq // tq
    hw = 2 * ATTN_HEAD_DIM
    wqk = in_w.astype(BF16)
    wvt = in_w[:, 2 * D_MODEL:].T.astype(BF16)
    cos_t, sin1_t, sin2_t = tables
    lane = np.arange(2 * hw)
    seg = jnp.asarray(lane[:, None] // ATTN_HEAD_DIM == lane[None, :] // ATTN_HEAD_DIM, BF16)
    rows_spec = pl.BlockSpec((tm, D_MODEL), lambda i: (i, 0))
    table_spec = pl.BlockSpec((tm, V7X_LANES), lambda i: (i, 0))
    qkv_shape = jax.ShapeDtypeStruct((m, D_MODEL), BF16)
    q, k, vt = pl.pallas_call(
        _attn_in_kernel,
        grid=(m // tm,),
        in_specs=[rows_spec, _const_spec((1, D_MODEL)), _const_spec(wqk.shape), _const_spec(wvt.shape),
                  table_spec, table_spec, table_spec, _const_spec((1, hw)), _const_spec((1, hw)),
                  _const_spec(seg.shape)],
        out_specs=[rows_spec, rows_spec,
                   pl.BlockSpec((tm // tk, ATTN_HEADS * ATTN_V_ROWS, tk), lambda i: (i, 0, 0))],
        out_shape=[qkv_shape, qkv_shape,
                   jax.ShapeDtypeStruct((m // tk, ATTN_HEADS * ATTN_V_ROWS, tk), BF16)],
        compiler_params=_params(("arbitrary",)), name="attn_in",
    )(x2, norm_g[None, :], wqk, wvt, cos_t, sin1_t, sin2_t, jnp.tile(q_norm_g, 2)[None, :],
      jnp.tile(k_norm_g, 2)[None, :], seg)

    vec64 = _const_spec((1, ATTN_HEAD_DIM))
    tile_spec = pl.BlockSpec((tq, D_MODEL), lambda b, qi: (b * qt + qi, 0))
    return pl.pallas_call(
        functools.partial(_flash_kernel, lambda_init=lambda_init),
        grid=(batch, qt),
        in_specs=[tile_spec,
                  pl.BlockSpec((seq, D_MODEL), lambda b, qi: (b, 0)),
                  pl.BlockSpec((seq // tk, ATTN_HEADS * ATTN_V_ROWS, tk), lambda b, qi: (b, 0, 0)),
                  tile_spec, vec64, vec64, vec64, vec64, _const_spec((1, ATTN_V_DIM)),
                  _const_spec((D_MODEL, D_MODEL))],
        out_specs=tile_spec,
        out_shape=jax.ShapeDtypeStruct((m, D_MODEL), F32),
        scratch_shapes=[pltpu.VMEM((ATTN_HEADS, ATTN_V_ROWS, 2 * tq), F32)],
        compiler_params=_params(("arbitrary", "arbitrary")), name="flash",
    )(q, k, vt, x2, lq1[None, :], lk1[None, :], lq2[None, :], lk2[None, :], subln_g[None, :],
      out_w.astype(BF16))


def _ffn_kernel(*refs, tiles_per_seq, n_casts):
    x_ref, g_ref, wup_ref, cw_ref, cb_ref, wd_ref = refs[:6]
    cast_in = refs[6:6 + n_casts]
    o_ref = refs[6 + n_casts]
    cast_out = refs[7 + n_casts:7 + 2 * n_casts]
    buf_g_ref, buf_u_ref, act_ref = refs[7 + 2 * n_casts:]
    first = pl.program_id(0) % tiles_per_seq == 0
    _conv_history_reset(buf_g_ref, first)
    _conv_history_reset(buf_u_ref, first)
    x = x_ref[...]
    h = _rmsnorm(x, g_ref[...]).astype(BF16)
    for c0 in range(0, D_FF, FFN_COLS):
        _conv_store(buf_g_ref, c0, _dot(h, wup_ref[:, c0:c0 + FFN_COLS]))
        _conv_store(buf_u_ref, c0, _dot(h, wup_ref[:, D_FF + c0:D_FF + c0 + FFN_COLS]))
        for l0 in range(c0, c0 + FFN_COLS, V7X_LANES):
            gl = slice(l0, l0 + V7X_LANES)
            ul = slice(D_FF + l0, D_FF + l0 + V7X_LANES)
            yg = _causal_conv(buf_g_ref, l0 // V7X_LANES, cw_ref[:, gl], cb_ref[:, gl])
            yu = _causal_conv(buf_u_ref, l0 // V7X_LANES, cw_ref[:, ul], cb_ref[:, ul])
            act_ref[:, gl] = (_silu(yg) * yu).astype(BF16)
    _conv_history_carry(buf_g_ref)
    _conv_history_carry(buf_u_ref)
    for src, dst in zip(cast_in, cast_out):
        dst[...] = src[...].astype(BF16)
    o_ref[...] = x + _dot(act_ref[...], wd_ref[...])


def _cast_blocking(rows, cols, n_steps):
    for ncb in (1, 2, 4, 8):
        nrb = n_steps // ncb
        if (rows % nrb == 0 and (rows // nrb) % V7X_BF16_SUBLANES == 0 and cols % ncb == 0
                and (ncb == 1 or (cols // ncb) % V7X_LANES == 0)):
            return rows // nrb, cols // ncb, ncb
    raise ValueError(f"no {n_steps}-block tiling for a ({rows}, {cols}) weight")


def _ffn_layer(x2, seq, norm_g, up_w, conv_w, conv_b, down_w, casts):
    m = x2.shape[0]
    tm = ROW_TILE
    steps = m // tm
    row = pl.BlockSpec((tm, D_MODEL), lambda i: (i, 0))
    cast_specs, cast_shapes = [], []
    for w, layer in casts:
        _, r, c = w.shape
        br, bc, ncb = _cast_blocking(r, c, steps)
        cast_specs.append((pl.BlockSpec((None, br, bc), lambda i, layer=layer, ncb=ncb: (layer, i // ncb, i % ncb)),
                           pl.BlockSpec((br, bc), lambda i, ncb=ncb: (i // ncb, i % ncb))))
        cast_shapes.append(jax.ShapeDtypeStruct((r, c), BF16))
    outs = pl.pallas_call(
        functools.partial(_ffn_kernel, tiles_per_seq=seq // tm, n_casts=len(casts)),
        grid=(steps,),
        in_specs=[row, _const_spec((1, D_MODEL)), _const_spec((D_MODEL, 2 * D_FF)), _const_spec((FFN_CONV, 2 * D_FF)),
                  _const_spec((1, 2 * D_FF)), _const_spec((D_FF, D_MODEL))] + [s[0] for s in cast_specs],
        out_specs=[row] + [s[1] for s in cast_specs],
        out_shape=[jax.ShapeDtypeStruct((m, D_MODEL), F32)] + cast_shapes,
        scratch_shapes=[pltpu.VMEM((D_FF // V7X_LANES, V7X_SUBLANES + tm, V7X_LANES), F32),
                        pltpu.VMEM((D_FF // V7X_LANES, V7X_SUBLANES + tm, V7X_LANES), F32),
                        pltpu.VMEM((tm, D_FF), BF16)],
        compiler_params=_params(("arbitrary",)), name="ffn",
    )(x2, norm_g[None, :], up_w, conv_w, conv_b[None, :], down_w, *[w for w, _ in casts])
    return outs[0], outs[1:]


def kernel(x, positions, norm_mix_g, norm_ffn_g, ssd_in_w, ssd_conv_w, ssd_conv_b, ssd_dt_bias, ssd_a_log, ssd_d, ssd_norm_g, ssd_out_w, attn_in_w, attn_q_norm_g, attn_k_norm_g, attn_lq1, attn_lk1, attn_lq2, attn_lk2, attn_subln_g, attn_out_w, ffn_up_w, ffn_conv_w, ffn_conv_b, ffn_down_w):
    batch, seq, d = x.shape
    depth = norm_mix_g.shape[0]
    assert d == D_MODEL and seq % ROW_TILE == 0 and ROW_TILE % ATTN_TILE == 0 and ATTN_TILE % CHUNK == 0
    x2 = x.reshape(batch * seq, d)
    tables = _rope_tables(positions)

    def mixer_weights(i):
        j = i // N_MIXERS
        return [(ssd_out_w, j)] if i % N_MIXERS == 0 else [(attn_in_w, j), (attn_out_w, j)]

    ssd_in_bf = ssd_in_w.astype(BF16)
    mix_w = [w[j].astype(BF16) for w, j in mixer_weights(0)]
    ffn_w = [ffn_up_w[0].astype(BF16), ffn_down_w[0].astype(BF16)]
    for i in range(depth):
        j = i // N_MIXERS
        if i % N_MIXERS == 0:
            x2 = _ssd_layer(x2, batch, seq, norm_mix_g[i], ssd_in_bf, j, ssd_conv_w[j], ssd_conv_b[j],
                            ssd_dt_bias[j], ssd_a_log[j], ssd_d[j], ssd_norm_g[j], mix_w[0])
        else:
            lambda_init = 0.8 - 0.6 * math.exp(-0.3 * i)
            x2 = _attn_layer(x2, batch, seq, tables, norm_mix_g[i], mix_w[0], attn_q_norm_g[j],
                             attn_k_norm_g[j], attn_lq1[j], attn_lk1[j], attn_lq2[j], attn_lk2[j],
                             attn_subln_g[j], mix_w[1], lambda_init)
        casts = mixer_weights(i + 1) + [(ffn_up_w, i + 1), (ffn_down_w, i + 1)] if i + 1 < depth else []
        x2, cast = _ffn_layer(x2, seq, norm_ffn_g[i], ffn_w[0], ffn_conv_w[i], ffn_conv_b[i], ffn_w[1], casts)
        mix_w, ffn_w = cast[:-2], cast[-2:]
    return x2.reshape(batch, seq, d)
```

```python
import functools
import math

import jax
import jax.numpy as jnp
import numpy as np
from jax import lax
from jax.experimental import pallas as pl
from jax.experimental.pallas import tpu as pltpu

F32 = jnp.float32
BF16 = jnp.bfloat16

V7X_LANES = 128
V7X_SUBLANES = 8
V7X_BF16_SUBLANES = 16
V7X_VMEM_LIMIT_BYTES = 56 * 1024 * 1024

D_MODEL = 1024
CHUNK = 64
EPS = 1e-6
SSD_D_INNER = 2048
SSD_HEAD_DIM = 64
SSD_HEADS = 32
SSD_GROUPS = 4
SSD_HEADS_PER_GROUP = 8
SSD_STATE = 128
SSD_CONV_CH = SSD_D_INNER + 2 * SSD_GROUPS * SSD_STATE
ATTN_HEADS = 8
ATTN_HEAD_DIM = 64
ATTN_V_DIM = 128
ATTN_V_ROWS = ATTN_V_DIM + V7X_BF16_SUBLANES
LOG2_E = math.log2(math.e)
ROPE_THETA = 500000.0
ROPE_DIM = 16
D_FF = 2816
FFN_CONV = 3
N_MIXERS = 2

ROW_TILE = 512
SSD_SCAN_CHUNK = 128
SSD_CONV_COLS = 512
SSD_OUT_ROWS = 256
FFN_COLS = 256
ATTN_TILE = 256


def _dot(a, b):
    return jnp.dot(a, b, preferred_element_type=F32)


def _dot_nt(a, b):
    return lax.dot_general(a, b, (((1,), (1,)), ((), ())), preferred_element_type=F32)


def _dot_tn(a, b):
    return lax.dot_general(a, b, (((0,), (0,)), ((), ())), preferred_element_type=F32)


def _silu(x):
    h = 0.5 * x
    return h + h * jnp.tanh(h)


def _rmsnorm(x, g):
    return x * lax.rsqrt(jnp.mean(x * x, axis=-1, keepdims=True) + EPS) * g


def _conv_store(buf_ref, c0, a):
    for j in range(a.shape[1] // V7X_LANES):
        buf_ref[c0 // V7X_LANES + j, V7X_SUBLANES:, :] = a[:, j * V7X_LANES:(j + 1) * V7X_LANES]


def _causal_conv(buf_ref, slab, w, b):
    kw = w.shape[0]
    tm = buf_ref.shape[1] - V7X_SUBLANES
    y = b + w[kw - 1:kw] * buf_ref[slab, V7X_SUBLANES:, :]
    for j in range(1, kw):
        y = y + w[kw - 1 - j:kw - j] * buf_ref[slab, V7X_SUBLANES - j:V7X_SUBLANES - j + tm, :]
    return y


def _conv_history_reset(buf_ref, first):
    @pl.when(first)
    def _():
        buf_ref[:, :V7X_SUBLANES, :] = jnp.zeros((buf_ref.shape[0], V7X_SUBLANES, V7X_LANES), buf_ref.dtype)


def _conv_history_carry(buf_ref):
    tm = buf_ref.shape[1] - V7X_SUBLANES
    buf_ref[:, :V7X_SUBLANES, :] = buf_ref[:, tm:, :]


def _const_spec(shape):
    nd = len(shape)
    return pl.BlockSpec(shape, lambda *_: (0,) * nd, pipeline_mode=pl.Buffered(1))


def _params(semantics):
    return pltpu.CompilerParams(dimension_semantics=semantics,
                                vmem_limit_bytes=V7X_VMEM_LIMIT_BYTES)


def _cast_blocks(src_refs, dst_refs):
    for src, dst in zip(src_refs, dst_refs):
        dst[...] = src[...].astype(BF16)


def _cast_blocking(rows, cols, n_steps):
    for ncb in (1, 2, 4, 8):
        nrb = n_steps // ncb
        if (rows % nrb == 0 and (rows // nrb) % V7X_BF16_SUBLANES == 0 and cols % ncb == 0
                and (ncb == 1 or (cols // ncb) % V7X_LANES == 0)):
            return rows // nrb, cols // ncb, ncb
    raise ValueError(f"no {n_steps}-block tiling for a ({rows}, {cols}) weight")


def _cast_specs(casts, n_steps):
    in_specs, out_specs, out_shapes = [], [], []
    for w, layer in casts:
        _, r, c = w.shape
        br, bc, ncb = _cast_blocking(r, c, n_steps)
        in_specs.append(pl.BlockSpec((None, br, bc), lambda i, layer=layer, ncb=ncb: (layer, i // ncb, i % ncb)))
        out_specs.append(pl.BlockSpec((br, bc), lambda i, ncb=ncb: (i // ncb, i % ncb)))
        out_shapes.append(jax.ShapeDtypeStruct((r, c), BF16))
    return in_specs, out_specs, out_shapes


def _rope_kernel(pos_ref, freq_ref, m1_ref, m2_ref, c_ref, s1_ref, s2_ref):
    ang = pos_ref[...].astype(F32) * freq_ref[...]
    s = jnp.sin(ang)
    c_ref[...] = jnp.cos(ang)
    s1_ref[...] = s * m1_ref[...]
    s2_ref[...] = s * m2_ref[...]


def _rope_tables(positions):
    b, s = positions.shape
    m = b * s
    inv_freq = ROPE_THETA ** (-jnp.arange(0, ROPE_DIM, 2, dtype=F32) / ROPE_DIM)
    d = jnp.arange(V7X_LANES) % ATTN_HEAD_DIM
    half = ROPE_DIM // 2
    freq = jnp.where(d < ROPE_DIM, inv_freq[d % half], 0.0).astype(F32)[None, :]
    m1 = jnp.where(d < half, -1.0, 0.0).astype(F32)[None, :]
    m2 = jnp.where((d >= half) & (d < ROPE_DIM), 1.0, 0.0).astype(F32)[None, :]
    pos = jnp.broadcast_to(positions.reshape(m, 1), (m, V7X_LANES))
    row = pl.BlockSpec((s, V7X_LANES), lambda i: (i, 0))
    vec = pl.BlockSpec((1, V7X_LANES), lambda i: (0, 0))
    out = jax.ShapeDtypeStruct((m, V7X_LANES), F32)
    return pl.pallas_call(
        _rope_kernel, grid=(b,), in_specs=[row, vec, vec, vec], out_specs=[row, row, row],
        out_shape=[out, out, out], compiler_params=_params(("arbitrary",)), name="rope_tables",
    )(pos, freq, m1, m2)


def _ssd_in_kernel(*refs, tiles_per_seq, n_casts):
    x_ref, g_ref, w_ref, cw_ref, cb_ref, dtb_ref = refs[:6]
    cast_in = refs[6:6 + n_casts]
    z_ref, xs_ref, b_ref, c_ref, dt_ref = refs[6 + n_casts:11 + n_casts]
    cast_out = refs[11 + n_casts:11 + 2 * n_casts]
    buf_ref = refs[11 + 2 * n_casts]
    _cast_blocks(cast_in, cast_out)
    _conv_history_reset(buf_ref, pl.program_id(0) % tiles_per_seq == 0)
    h = _rmsnorm(x_ref[...], g_ref[...]).astype(BF16)
    gn = SSD_GROUPS * SSD_STATE
    n_chunks = SSD_CONV_CH // SSD_CONV_COLS
    zw = SSD_CONV_COLS
    xbc0 = SSD_D_INNER
    dt0 = SSD_D_INNER + SSD_CONV_CH

    def project_z(c):
        if (c + 1) * zw <= SSD_D_INNER:
            z_ref[:, c * zw:(c + 1) * zw] = _dot(h, w_ref[:, c * zw:(c + 1) * zw]).astype(BF16)
        elif c * zw == SSD_D_INNER:
            dt = _dot(h, w_ref[:, dt0:]) + dtb_ref[...]
            dt_ref[...] = jnp.zeros(dt_ref.shape, F32)
            dt_ref[:, :SSD_HEADS] = jnp.maximum(dt, 0.0) + jnp.log1p(jnp.exp(-jnp.abs(dt)))

    _conv_store(buf_ref, 0, _dot(h, w_ref[:, xbc0:xbc0 + SSD_CONV_COLS]))
    for c in range(n_chunks):
        c0 = c * SSD_CONV_COLS
        if c + 1 < n_chunks:
            _conv_store(buf_ref, c0 + SSD_CONV_COLS,
                        _dot(h, w_ref[:, xbc0 + c0 + SSD_CONV_COLS:xbc0 + c0 + 2 * SSD_CONV_COLS]))
        project_z(c)
        for l0 in range(c0, c0 + SSD_CONV_COLS, V7X_LANES):
            lanes = slice(l0, l0 + V7X_LANES)
            y = _silu(_causal_conv(buf_ref, l0 // V7X_LANES, cw_ref[:, lanes], cb_ref[:, lanes])).astype(BF16)
            if l0 < SSD_D_INNER:
                xs_ref[:, lanes] = y
            elif l0 < SSD_D_INNER + gn:
                b_ref[:, l0 - SSD_D_INNER:l0 - SSD_D_INNER + V7X_LANES] = y
            else:
                c_ref[:, l0 - SSD_D_INNER - gn:l0 - SSD_D_INNER - gn + V7X_LANES] = y
    _conv_history_carry(buf_ref)


def _ssd_core_kernel(xs_ref, b_ref, c_ref, dt_ref, z_ref, x_ref, alog_ref, dskip_ref, e_ref,
                     ng_ref, wo_ref, o_ref, h_ref, y_ref):
    ts = xs_ref.shape[0]
    L = SSD_SCAN_CHUNK
    P2 = 2 * SSD_HEAD_DIM
    GP = SSD_HEADS_PER_GROUP * SSD_HEAD_DIM

    @pl.when(pl.program_id(1) == 0)
    def _():
        h_ref[...] = jnp.zeros_like(h_ref)

    row = lax.broadcasted_iota(jnp.int32, (L, L), 0)
    col = lax.broadcasted_iota(jnp.int32, (L, L), 1)
    causal = row >= col
    tril = jnp.where(causal, 1.0, 0.0).astype(BF16)
    left = lax.broadcasted_iota(jnp.int32, (L, P2), 1) < SSD_HEAD_DIM
    a_row = -jnp.exp(alog_ref[...])

    def chunk_body(ci, carry):
        rows = pl.ds(pl.multiple_of(ci * L, L), L)
        dt = dt_ref[rows, :]
        da = dt * a_row
        hi = da.astype(BF16)
        r1 = da - hi.astype(F32)
        mid = r1.astype(BF16)
        lo = (r1 - mid.astype(F32)).astype(BF16)
        cs = _dot(tril, hi) + _dot(tril, mid) + _dot(tril, lo)
        cs_t = cs.T
        ecs = jnp.exp(cs)
        dec = jnp.exp(cs[L - 1:L, :] - cs)
        stacked = jnp.concatenate([dt, dec, ecs], axis=0).astype(BF16)
        wide = _dot(stacked, e_ref[...])
        dt_e, dec_e, ecs_e = wide[:L], wide[L:2 * L], wide[2 * L:]
        xs = xs_ref[rows, :].astype(F32)
        xdt = xs * dt_e
        xdt_b = xdt.astype(BF16)
        dx_b = (xdt * dec_e).astype(BF16)
        for g in range(SSD_GROUPS):
            gc = slice(g * SSD_STATE, (g + 1) * SSD_STATE)
            hc = slice(g * GP, (g + 1) * GP)
            bg = b_ref[rows, gc]
            cg = c_ref[rows, gc]
            cb = _dot_nt(cg, bg)
            hst = h_ref[g]
            y_off = _dot(cg, hst.astype(BF16)) * ecs_e[:, hc]
            h_ref[g] = hst * ecs_e[L - 1:L, hc] + _dot_tn(bg, dx_b[:, hc])
            for p in range(SSD_HEADS_PER_GROUP // 2):
                h0 = g * SSD_HEADS_PER_GROUP + 2 * p
                pc = slice(h0 * SSD_HEAD_DIM, h0 * SSD_HEAD_DIM + P2)
                pair = xdt_b[:, pc]

                def lmat(hd):
                    seg = cs[:, hd:hd + 1] - cs_t[hd:hd + 1, :]
                    return (cb * jnp.exp(jnp.where(causal, seg, -jnp.inf))).astype(BF16)

                lhs = jnp.concatenate([lmat(h0), lmat(h0 + 1)], axis=1)
                rhs = jnp.concatenate([jnp.where(left, pair, 0), jnp.where(left, 0, pair)], axis=0)
                y_ref[rows, pc] = (_dot(lhs, rhs) + y_off[:, p * P2:(p + 1) * P2]
                                   + xs[:, pc] * dskip_ref[:, pc])
        return carry

    lax.fori_loop(0, ts // L, chunk_body, 0, unroll=True)

    gw = SSD_D_INNER // SSD_GROUPS
    for r0 in range(0, ts, SSD_OUT_ROWS):
        rows = slice(r0, r0 + SSD_OUT_ROWS)
        gated = y_ref[rows, :] * _silu(z_ref[rows, :].astype(F32))
        parts = []
        for g in range(SSD_GROUPS):
            gg = gated[:, g * gw:(g + 1) * gw]
            parts.append(gg * lax.rsqrt(jnp.mean(gg * gg, axis=-1, keepdims=True) + EPS))
        yn = (jnp.concatenate(parts, axis=-1) * ng_ref[...]).astype(BF16)
        o_ref[rows, :] = x_ref[rows, :] + _dot(yn, wo_ref[...])


def _ssd_layer(x2, batch, seq, norm_g, in_w_layers, layer, conv_w, conv_b, dt_bias, a_log, d_skip, ssd_norm_g,
               out_w, casts):
    m = x2.shape[0]
    tm = ROW_TILE
    gn = SSD_GROUPS * SSD_STATE
    alog = jnp.pad(a_log, (0, V7X_LANES - SSD_HEADS))[None, :]

    def rows(width):
        return pl.BlockSpec((tm, width), lambda i: (i, 0))

    cast_in, cast_out, cast_shapes = _cast_specs(casts, m // tm)
    z, xs, bm, cm, dt, *cast = pl.pallas_call(
        functools.partial(_ssd_in_kernel, tiles_per_seq=seq // tm, n_casts=len(casts)),
        grid=(m // tm,),
        in_specs=[rows(D_MODEL), _const_spec((1, D_MODEL)),
                  pl.BlockSpec((None,) + in_w_layers.shape[1:], lambda i: (layer, 0, 0),
                               pipeline_mode=pl.Buffered(1)),
                  _const_spec(conv_w.shape), _const_spec((1, SSD_CONV_CH)), _const_spec((1, SSD_HEADS))] + cast_in,
        out_specs=[rows(SSD_D_INNER), rows(SSD_D_INNER), rows(gn), rows(gn), rows(V7X_LANES)] + cast_out,
        out_shape=[jax.ShapeDtypeStruct((m, SSD_D_INNER), BF16), jax.ShapeDtypeStruct((m, SSD_D_INNER), BF16),
                   jax.ShapeDtypeStruct((m, gn), BF16), jax.ShapeDtypeStruct((m, gn), BF16),
                   jax.ShapeDtypeStruct((m, V7X_LANES), F32)] + cast_shapes,
        scratch_shapes=[pltpu.VMEM((SSD_CONV_CH // V7X_LANES, V7X_SUBLANES + tm, V7X_LANES), F32)],
        compiler_params=_params(("arbitrary",)), name="ssd_in",
    )(x2, norm_g[None, :], in_w_layers, conv_w, conv_b[None, :], dt_bias[None, :], *[w for w, _ in casts])
    if out_w is None:
        out_w = cast.pop(0)

    expand = jnp.asarray(np.arange(V7X_LANES)[:, None] == (np.arange(SSD_D_INNER) // SSD_HEAD_DIM)[None, :], BF16)
    dskip = jnp.repeat(d_skip, SSD_HEAD_DIM)[None, :]
    ts = ROW_TILE
    spt = seq // ts

    def tile(width):
        return pl.BlockSpec((ts, width), lambda b, s: (b * spt + s, 0))

    out = pl.pallas_call(
        _ssd_core_kernel,
        grid=(batch, spt),
        in_specs=[tile(SSD_D_INNER), tile(gn), tile(gn), tile(V7X_LANES), tile(SSD_D_INNER), tile(D_MODEL),
                  _const_spec((1, V7X_LANES)), _const_spec((1, SSD_D_INNER)), _const_spec(expand.shape),
                  _const_spec((1, SSD_D_INNER)), _const_spec((SSD_D_INNER, D_MODEL))],
        out_specs=tile(D_MODEL),
        out_shape=jax.ShapeDtypeStruct((m, D_MODEL), F32),
        scratch_shapes=[pltpu.VMEM((SSD_GROUPS, SSD_STATE, SSD_HEADS_PER_GROUP * SSD_HEAD_DIM), F32),
                        pltpu.VMEM((ts, SSD_D_INNER), F32)],
        compiler_params=_params(("arbitrary", "arbitrary")), name="ssd_core",
    )(xs, bm, cm, dt, z, x2, alog, dskip, expand, ssd_norm_g[None, :], out_w)
    return out, cast


def _attn_in_kernel(x_ref, g_ref, wqk_ref, wvt_ref, c_ref, s1_ref, s2_ref, qg_ref, kg_ref, seg_ref,
                    q_ref, k_ref, vt_ref):
    t = ATTN_TILE
    hw = 2 * ATTN_HEAD_DIM
    half = ROPE_DIM // 2
    tm = x_ref.shape[0]
    h = _rmsnorm(x_ref[...], g_ref[...]).astype(BF16)
    cos_t, sin1_t, sin2_t = c_ref[...], s1_ref[...], s2_ref[...]
    q_scale = LOG2_E / math.sqrt(ATTN_HEAD_DIM)
    for c0, o_ref, gain_ref, scale in ((0, q_ref, qg_ref, q_scale), (D_MODEL, k_ref, kg_ref, None)):
        a = _dot(h, wqk_ref[:, c0:c0 + D_MODEL])
        for p0 in range(0, D_MODEL, 2 * hw):
            pair = a[:, p0:p0 + 2 * hw]
            ss = _dot((pair * pair).astype(BF16), seg_ref[...])
            inv = lax.rsqrt(ss * (1.0 / ATTN_HEAD_DIM) + EPS)
            for j in range(2):
                lanes = slice(j * hw, (j + 1) * hw)
                xn = pair[:, lanes] * inv[:, lanes] * gain_ref[...]
                y = (xn * cos_t + pltpu.roll(xn, hw - half, axis=1) * sin1_t
                     + pltpu.roll(xn, half, axis=1) * sin2_t)
                o_ref[:, p0 + j * hw:p0 + (j + 1) * hw] = (y if scale is None else y * scale).astype(BF16)
    vt = _dot_nt(wvt_ref[...], h).astype(BF16)
    ones = jnp.ones((ATTN_V_ROWS - ATTN_V_DIM, tm), BF16)
    pieces = []
    for g in range(ATTN_HEADS):
        pieces += [vt[g * ATTN_V_DIM:(g + 1) * ATTN_V_DIM], ones]
    vt = jnp.concatenate(pieces, axis=0)
    for j in range(tm // t):
        vt_ref[j] = vt[:, j * t:(j + 1) * t]


def _flash_kernel(q_ref, k_ref, vt_ref, x_ref, lq1_ref, lk1_ref, lq2_ref, lk2_ref, sg_ref, wo_ref,
                  o_ref, acc_ref, *, lambda_init):
    t = ATTN_TILE
    hw = 2 * ATTN_HEAD_DIM
    qi = pl.program_id(1)
    left_row = lax.broadcasted_iota(jnp.int32, (1, V7X_LANES), 1) < ATTN_HEAD_DIM
    heads = range(ATTN_HEADS)

    qa, qb = [], []
    for g in heads:
        qh = q_ref[:, g * hw:(g + 1) * hw]
        qa.append(jnp.where(left_row, qh, 0))
        qb.append(jnp.where(left_row, 0, qh))

    def kv_tile(kt, carry, masked=False):
        rows = pl.ds(pl.multiple_of(kt * t, t), t)
        scores = []
        for g in heads:
            ks = k_ref[rows, g * hw:(g + 1) * hw]
            scores.append((_dot_nt(ks, qa[g]), _dot_nt(ks, qb[g])))
        if masked:
            kc = lax.broadcasted_iota(jnp.int32, (t, t), 0) // CHUNK
            qc = lax.broadcasted_iota(jnp.int32, (t, t), 1) // CHUNK
            allowed = kc <= qc
            scores = [tuple(jnp.where(allowed, s, -jnp.inf) for s in pair) for pair in scores]
        probs, out = [], []
        for g in heads:
            new, alphas, ps = [], [], []
            for c, s in enumerate(scores[g]):
                m = jnp.max(s, axis=0, keepdims=True)
                if carry is not None:
                    m_old = carry[g][c]
                    m = jnp.maximum(m_old, m)
                    alphas.append(jnp.exp2(m_old - m))
                ps.append(jnp.exp2(s - m))
                new.append(m)
            probs.append((jnp.concatenate(ps, axis=1).astype(BF16), alphas))
            out.append(tuple(new))
        for g in heads:
            p, alphas = probs[g]
            pv = _dot(vt_ref[kt, g * ATTN_V_ROWS:(g + 1) * ATTN_V_ROWS, :], p)
            if carry is None:
                acc_ref[g] = pv
            else:
                acc_ref[g] = acc_ref[g] * jnp.concatenate(alphas, axis=1) + pv
        return tuple(out)

    lax.fori_loop(0, qi, kv_tile, kv_tile(qi, None, True))

    lam = (jnp.exp(jnp.sum(lq1_ref[...] * lk1_ref[...], axis=-1, keepdims=True))
           - jnp.exp(jnp.sum(lq2_ref[...] * lk2_ref[...], axis=-1, keepdims=True)) + lambda_init)
    outs = []
    for g in heads:
        acc = acc_ref[g]
        la = acc[ATTN_V_DIM:ATTN_V_DIM + 1, :t]
        lb = acc[ATTN_V_DIM:ATTN_V_DIM + 1, t:]
        ot = acc[:ATTN_V_DIM, :t] / la - lam * (acc[:ATTN_V_DIM, t:] / lb)
        ot = ot * lax.rsqrt(jnp.mean(ot * ot, axis=0, keepdims=True) + EPS)
        outs.append((ot.T * sg_ref[...] * (1.0 - lambda_init)).astype(BF16))
    o_ref[...] = x_ref[...] + _dot(jnp.concatenate(outs, axis=1), wo_ref[...])


def _attn_layer(x2, batch, seq, tables, norm_g, in_w, q_norm_g, k_norm_g, lq1, lk1, lq2, lk2,
                subln_g, out_w, lambda_init):
    m = x2.shape[0]
    tm = ROW_TILE
    tk = tq = ATTN_TILE
    qt = seq // tq
    hw = 2 * ATTN_HEAD_DIM
    wqk = in_w.astype(BF16)
    wvt = in_w[:, 2 * D_MODEL:].T.astype(BF16)
    cos_t, sin1_t, sin2_t = tables
    lane = np.arange(2 * hw)
    seg = jnp.asarray(lane[:, None] // ATTN_HEAD_DIM == lane[None, :] // ATTN_HEAD_DIM, BF16)
    rows_spec = pl.BlockSpec((tm, D_MODEL), lambda i: (i, 0))
    table_spec = pl.BlockSpec((tm, V7X_LANES), lambda i: (i, 0))
    qkv_shape = jax.ShapeDtypeStruct((m, D_MODEL), BF16)
    q, k, vt = pl.pallas_call(
        _attn_in_kernel,
        grid=(m // tm,),
        in_specs=[rows_spec, _const_spec((1, D_MODEL)), _const_spec(wqk.shape), _const_spec(wvt.shape),
                  table_spec, table_spec, table_spec, _const_spec((1, hw)), _const_spec((1, hw)),
                  _const_spec(seg.shape)],
        out_specs=[rows_spec, rows_spec,
                   pl.BlockSpec((tm // tk, ATTN_HEADS * ATTN_V_ROWS, tk), lambda i: (i, 0, 0))],
        out_shape=[qkv_shape, qkv_shape,
                   jax.ShapeDtypeStruct((m // tk, ATTN_HEADS * ATTN_V_ROWS, tk), BF16)],
        compiler_params=_params(("arbitrary",)), name="attn_in",
    )(x2, norm_g[None, :], wqk, wvt, cos_t, sin1_t, sin2_t, jnp.tile(q_norm_g, 2)[None, :],
      jnp.tile(k_norm_g, 2)[None, :], seg)

    vec64 = _const_spec((1, ATTN_HEAD_DIM))
    tile_spec = pl.BlockSpec((tq, D_MODEL), lambda b, qi: (b * qt + qi, 0))
    return pl.pallas_call(
        functools.partial(_flash_kernel, lambda_init=lambda_init),
        grid=(batch, qt),
        in_specs=[tile_spec,
                  pl.BlockSpec((seq, D_MODEL), lambda b, qi: (b, 0)),
                  pl.BlockSpec((seq // tk, ATTN_HEADS * ATTN_V_ROWS, tk), lambda b, qi: (b, 0, 0)),
                  tile_spec, vec64, vec64, vec64, vec64, _const_spec((1, ATTN_V_DIM)),
                  _const_spec((D_MODEL, D_MODEL))],
        out_specs=tile_spec,
        out_shape=jax.ShapeDtypeStruct((m, D_MODEL), F32),
        scratch_shapes=[pltpu.VMEM((ATTN_HEADS, ATTN_V_ROWS, 2 * tq), F32)],
        compiler_params=_params(("arbitrary", "arbitrary")), name="flash",
    )(q, k, vt, x2, lq1[None, :], lk1[None, :], lq2[None, :], lk2[None, :], subln_g[None, :],
      out_w.astype(BF16))


def _ffn_kernel(*refs, tiles_per_seq, n_casts):
    x_ref, g_ref, wup_ref, cw_ref, cb_ref, wd_ref = refs[:6]
    cast_in = refs[6:6 + n_casts]
    o_ref = refs[6 + n_casts]
    cast_out = refs[7 + n_casts:7 + 2 * n_casts]
    buf_g_ref, buf_u_ref, act_ref = refs[7 + 2 * n_casts:]
    first = pl.program_id(0) % tiles_per_seq == 0
    _conv_history_reset(buf_g_ref, first)
    _conv_history_reset(buf_u_ref, first)
    x = x_ref[...]
    h = _rmsnorm(x, g_ref[...]).astype(BF16)
    for c0 in range(0, D_FF, FFN_COLS):
        _conv_store(buf_g_ref, c0, _dot(h, wup_ref[:, c0:c0 + FFN_COLS]))
        _conv_store(buf_u_ref, c0, _dot(h, wup_ref[:, D_FF + c0:D_FF + c0 + FFN_COLS]))
        for l0 in range(c0, c0 + FFN_COLS, V7X_LANES):
            gl = slice(l0, l0 + V7X_LANES)
            ul = slice(D_FF + l0, D_FF + l0 + V7X_LANES)
            yg = _causal_conv(buf_g_ref, l0 // V7X_LANES, cw_ref[:, gl], cb_ref[:, gl])
            yu = _causal_conv(buf_u_ref, l0 // V7X_LANES, cw_ref[:, ul], cb_ref[:, ul])
            act_ref[:, gl] = (_silu(yg) * yu).astype(BF16)
    _conv_history_carry(buf_g_ref)
    _conv_history_carry(buf_u_ref)
    _cast_blocks(cast_in, cast_out)
    o_ref[...] = x + _dot(act_ref[...], wd_ref[...])


def _ffn_layer(x2, seq, norm_g, up_w, conv_w, conv_b, down_w, casts):
    m = x2.shape[0]
    tm = ROW_TILE
    steps = m // tm
    row = pl.BlockSpec((tm, D_MODEL), lambda i: (i, 0))
    cast_in, cast_out, cast_shapes = _cast_specs(casts, steps)
    outs = pl.pallas_call(
        functools.partial(_ffn_kernel, tiles_per_seq=seq // tm, n_casts=len(casts)),
        grid=(steps,),
        in_specs=[row, _const_spec((1, D_MODEL)), _const_spec((D_MODEL, 2 * D_FF)), _const_spec((FFN_CONV, 2 * D_FF)),
                  _const_spec((1, 2 * D_FF)), _const_spec((D_FF, D_MODEL))] + cast_in,
        out_specs=[row] + cast_out,
        out_shape=[jax.ShapeDtypeStruct((m, D_MODEL), F32)] + cast_shapes,
        scratch_shapes=[pltpu.VMEM((D_FF // V7X_LANES, V7X_SUBLANES + tm, V7X_LANES), F32),
                        pltpu.VMEM((D_FF // V7X_LANES, V7X_SUBLANES + tm, V7X_LANES), F32),
                        pltpu.VMEM((tm, D_FF), BF16)],
        compiler_params=_params(("arbitrary",)), name="ffn",
    )(x2, norm_g[None, :], up_w, conv_w, conv_b[None, :], down_w, *[w for w, _ in casts])
    return outs[0], outs[1:]


def kernel(x, positions, norm_mix_g, norm_ffn_g, ssd_in_w, ssd_conv_w, ssd_conv_b, ssd_dt_bias, ssd_a_log, ssd_d, ssd_norm_g, ssd_out_w, attn_in_w, attn_q_norm_g, attn_k_norm_g, attn_lq1, attn_lk1, attn_lq2, attn_lk2, attn_subln_g, attn_out_w, ffn_up_w, ffn_conv_w, ffn_conv_b, ffn_down_w):
    batch, seq, d = x.shape
    depth = norm_mix_g.shape[0]
    assert d == D_MODEL and seq % ROW_TILE == 0 and ROW_TILE % ATTN_TILE == 0 and ATTN_TILE % CHUNK == 0
    assert ssd_conv_w.shape[1] - 1 <= V7X_SUBLANES and ffn_conv_w.shape[1] - 1 <= V7X_SUBLANES
    x2 = x.reshape(batch * seq, d)
    tables = _rope_tables(positions)

    def mixer_weights(i):
        j = i // N_MIXERS
        return [(ssd_out_w, j)] if i % N_MIXERS == 0 else [(attn_in_w, j), (attn_out_w, j)]

    ssd_in_bf = ssd_in_w.astype(BF16)
    mix_w, ffn_w = [None], None
    for i in range(depth):
        j = i // N_MIXERS
        if i % N_MIXERS == 0:
            first_casts = [(ssd_out_w, 0), (ffn_up_w, 0), (ffn_down_w, 0)] if i == 0 else []
            x2, cast = _ssd_layer(x2, batch, seq, norm_mix_g[i], ssd_in_bf, j, ssd_conv_w[j], ssd_conv_b[j],
                                  ssd_dt_bias[j], ssd_a_log[j], ssd_d[j], ssd_norm_g[j], mix_w[0], first_casts)
            if i == 0:
                ffn_w = cast
        else:
            lambda_init = 0.8 - 0.6 * math.exp(-0.3 * i)
            x2 = _attn_layer(x2, batch, seq, tables, norm_mix_g[i], mix_w[0], attn_q_norm_g[j],
                             attn_k_norm_g[j], attn_lq1[j], attn_lk1[j], attn_lq2[j], attn_lk2[j],
                             attn_subln_g[j], mix_w[1], lambda_init)
        casts = mixer_weights(i + 1) + [(ffn_up_w, i + 1), (ffn_down_w, i + 1)] if i + 1 < depth else []
        x2, cast = _ffn_layer(x2, seq, norm_ffn_g[i], ffn_w[0], ffn_conv_w[i], ffn_conv_b[i], ffn_w[1], casts)
        mix_w, ffn_w = cast[:-2], cast[-2:]
    return x2.reshape(batch, seq, d)
```

```python
import functools
import math

import jax
import jax.numpy as jnp
import numpy as np
from jax import lax
from jax.experimental import pallas as pl
from jax.experimental.pallas import tpu as pltpu

F32 = jnp.float32
BF16 = jnp.bfloat16

V7X_LANES = 128
V7X_SUBLANES = 8
V7X_BF16_SUBLANES = 16
V7X_VMEM_LIMIT_BYTES = 56 * 1024 * 1024

D_MODEL = 1024
CHUNK = 64
EPS = 1e-6
SSD_D_INNER = 2048
SSD_HEAD_DIM = 64
SSD_HEADS = 32
SSD_GROUPS = 4
SSD_HEADS_PER_GROUP = 8
SSD_STATE = 128
SSD_CONV_CH = SSD_D_INNER + 2 * SSD_GROUPS * SSD_STATE
ATTN_HEADS = 8
ATTN_HEAD_DIM = 64
ATTN_V_DIM = 128
ATTN_V_ROWS = ATTN_V_DIM + V7X_BF16_SUBLANES
LOG2_E = math.log2(math.e)
ROPE_THETA = 500000.0
ROPE_DIM = 16
D_FF = 2816
FFN_CONV = 3
N_MIXERS = 2

ROW_TILE = 512
SSD_SCAN_CHUNK = 128
SSD_CONV_COLS = 512
SSD_OUT_ROWS = 256
FFN_COLS = 256
ATTN_TILE = 256


def _dot(a, b):
    return jnp.dot(a, b, preferred_element_type=F32)


def _dot_nt(a, b):
    return lax.dot_general(a, b, (((1,), (1,)), ((), ())), preferred_element_type=F32)


def _dot_tn(a, b):
    return lax.dot_general(a, b, (((0,), (0,)), ((), ())), preferred_element_type=F32)


def _silu(x):
    h = 0.5 * x
    return h + h * jnp.tanh(h)


def _rmsnorm(x, g):
    return x * lax.rsqrt(jnp.mean(x * x, axis=-1, keepdims=True) + EPS) * g


def _conv_store(buf_ref, c0, a):
    for j in range(a.shape[1] // V7X_LANES):
        buf_ref[c0 // V7X_LANES + j, V7X_SUBLANES:, :] = a[:, j * V7X_LANES:(j + 1) * V7X_LANES]


def _causal_conv(buf_ref, slab, w, b):
    kw = w.shape[0]
    tm = buf_ref.shape[1] - V7X_SUBLANES
    y = b + w[kw - 1:kw] * buf_ref[slab, V7X_SUBLANES:, :]
    for j in range(1, kw):
        y = y + w[kw - 1 - j:kw - j] * buf_ref[slab, V7X_SUBLANES - j:V7X_SUBLANES - j + tm, :]
    return y


def _conv_history_reset(buf_ref, first):
    @pl.when(first)
    def _():
        buf_ref[:, :V7X_SUBLANES, :] = jnp.zeros((buf_ref.shape[0], V7X_SUBLANES, V7X_LANES), buf_ref.dtype)


def _conv_history_carry(buf_ref):
    tm = buf_ref.shape[1] - V7X_SUBLANES
    buf_ref[:, :V7X_SUBLANES, :] = buf_ref[:, tm:, :]


def _const_spec(shape):
    nd = len(shape)
    return pl.BlockSpec(shape, lambda *_: (0,) * nd, pipeline_mode=pl.Buffered(1))


def _params(semantics):
    return pltpu.CompilerParams(dimension_semantics=semantics,
                                vmem_limit_bytes=V7X_VMEM_LIMIT_BYTES)


def _cast_blocks(src_refs, dst_refs):
    for src, dst in zip(src_refs, dst_refs):
        dst[...] = src[...].astype(BF16)


def _cast_blocking(rows, cols, n_steps):
    for ncb in (1, 2, 4, 8):
        nrb = n_steps // ncb
        if (rows % nrb == 0 and (rows // nrb) % V7X_BF16_SUBLANES == 0 and cols % ncb == 0
                and (ncb == 1 or (cols // ncb) % V7X_LANES == 0)):
            return rows // nrb, cols // ncb, ncb
    raise ValueError(f"no {n_steps}-block tiling for a ({rows}, {cols}) weight")


def _cast_specs(casts, n_steps):
    in_specs, out_specs, out_shapes = [], [], []
    for w, layer in casts:
        _, r, c = w.shape
        br, bc, ncb = _cast_blocking(r, c, n_steps)
        in_specs.append(pl.BlockSpec((None, br, bc), lambda i, layer=layer, ncb=ncb: (layer, i // ncb, i % ncb)))
        out_specs.append(pl.BlockSpec((br, bc), lambda i, ncb=ncb: (i // ncb, i % ncb)))
        out_shapes.append(jax.ShapeDtypeStruct((r, c), BF16))
    return in_specs, out_specs, out_shapes


def _rope_kernel(pos_ref, c_ref, s1_ref, s2_ref):
    seq = pos_ref.shape[1]
    half = ROPE_DIM // 2
    pos = jnp.broadcast_to(pos_ref[...].astype(F32), (V7X_LANES, seq)).T
    d = lax.broadcasted_iota(jnp.int32, (1, V7X_LANES), 1) % ATTN_HEAD_DIM
    rotary = d < ROPE_DIM
    inv_freq = ROPE_THETA ** (-(2 * (d % half)).astype(F32) / ROPE_DIM)
    ang = pos * jnp.where(rotary, inv_freq, 0.0)
    s = jnp.sin(ang)
    c_ref[...] = jnp.cos(ang)
    s1_ref[...] = s * jnp.where(d < half, -1.0, 0.0)
    s2_ref[...] = s * jnp.where(rotary & (d >= half), 1.0, 0.0)


def _rope_tables(positions):
    b, s = positions.shape
    row = pl.BlockSpec((s, V7X_LANES), lambda i: (i, 0))
    out = jax.ShapeDtypeStruct((b * s, V7X_LANES), F32)
    return pl.pallas_call(
        _rope_kernel, grid=(b,), in_specs=[pl.BlockSpec((None, 1, s), lambda i: (i, 0, 0))],
        out_specs=[row, row, row], out_shape=[out, out, out],
        compiler_params=_params(("arbitrary",)), name="rope_tables",
    )(positions.reshape(b, 1, s))


def _ssd_in_kernel(*refs, tiles_per_seq, n_casts):
    x_ref, g_ref, w_ref, cw_ref, cb_ref, dtb_ref = refs[:6]
    cast_in = refs[6:6 + n_casts]
    z_ref, xs_ref, b_ref, c_ref, dt_ref = refs[6 + n_casts:11 + n_casts]
    cast_out = refs[11 + n_casts:11 + 2 * n_casts]
    buf_ref = refs[11 + 2 * n_casts]
    _cast_blocks(cast_in, cast_out)
    _conv_history_reset(buf_ref, pl.program_id(0) % tiles_per_seq == 0)
    h = _rmsnorm(x_ref[...], g_ref[...]).astype(BF16)
    gn = SSD_GROUPS * SSD_STATE
    n_chunks = SSD_CONV_CH // SSD_CONV_COLS
    zw = SSD_CONV_COLS
    xbc0 = SSD_D_INNER
    dt0 = SSD_D_INNER + SSD_CONV_CH

    def project_z(c):
        if (c + 1) * zw <= SSD_D_INNER:
            z_ref[:, c * zw:(c + 1) * zw] = _dot(h, w_ref[:, c * zw:(c + 1) * zw]).astype(BF16)
        elif c * zw == SSD_D_INNER:
            dt = _dot(h, w_ref[:, dt0:]) + dtb_ref[...]
            dt_ref[...] = jnp.zeros(dt_ref.shape, F32)
            dt_ref[:, :SSD_HEADS] = jnp.maximum(dt, 0.0) + jnp.log1p(jnp.exp(-jnp.abs(dt)))

    _conv_store(buf_ref, 0, _dot(h, w_ref[:, xbc0:xbc0 + SSD_CONV_COLS]))
    for c in range(n_chunks):
        c0 = c * SSD_CONV_COLS
        if c + 1 < n_chunks:
            _conv_store(buf_ref, c0 + SSD_CONV_COLS,
                        _dot(h, w_ref[:, xbc0 + c0 + SSD_CONV_COLS:xbc0 + c0 + 2 * SSD_CONV_COLS]))
        project_z(c)
        for l0 in range(c0, c0 + SSD_CONV_COLS, V7X_LANES):
            lanes = slice(l0, l0 + V7X_LANES)
            y = _silu(_causal_conv(buf_ref, l0 // V7X_LANES, cw_ref[:, lanes], cb_ref[:, lanes])).astype(BF16)
            if l0 < SSD_D_INNER:
                xs_ref[:, lanes] = y
            elif l0 < SSD_D_INNER + gn:
                b_ref[:, l0 - SSD_D_INNER:l0 - SSD_D_INNER + V7X_LANES] = y
            else:
                c_ref[:, l0 - SSD_D_INNER - gn:l0 - SSD_D_INNER - gn + V7X_LANES] = y
    _conv_history_carry(buf_ref)


def _ssd_core_kernel(xs_ref, b_ref, c_ref, dt_ref, z_ref, x_ref, alog_ref, dskip_ref, e_ref,
                     ng_ref, wo_ref, o_ref, h_ref, y_ref):
    ts = xs_ref.shape[0]
    L = SSD_SCAN_CHUNK
    P2 = 2 * SSD_HEAD_DIM
    GP = SSD_HEADS_PER_GROUP * SSD_HEAD_DIM

    @pl.when(pl.program_id(1) == 0)
    def _():
        h_ref[...] = jnp.zeros_like(h_ref)

    row = lax.broadcasted_iota(jnp.int32, (L, L), 0)
    col = lax.broadcasted_iota(jnp.int32, (L, L), 1)
    causal = row >= col
    tril = jnp.where(causal, 1.0, 0.0).astype(BF16)
    left = lax.broadcasted_iota(jnp.int32, (L, P2), 1) < SSD_HEAD_DIM
    a_row = -jnp.exp(alog_ref[...])

    def chunk_body(ci, carry):
        rows = pl.ds(pl.multiple_of(ci * L, L), L)
        dt = dt_ref[rows, :]
        da = dt * a_row
        hi = da.astype(BF16)
        r1 = da - hi.astype(F32)
        mid = r1.astype(BF16)
        lo = (r1 - mid.astype(F32)).astype(BF16)
        cs = (_dot(tril, hi) + _dot(tril, mid) + _dot(tril, lo)) * LOG2_E
        cs_t = cs.T
        ecs = jnp.exp2(cs)
        dec = jnp.exp2(cs[L - 1:L, :] - cs)
        stacked = jnp.concatenate([dt, dec, ecs], axis=0).astype(BF16)
        wide = _dot(stacked, e_ref[...])
        dt_e, dec_e, ecs_e = wide[:L], wide[L:2 * L], wide[2 * L:]
        xs = xs_ref[rows, :].astype(F32)
        xdt = xs * dt_e
        xdt_b = xdt.astype(BF16)
        dx_b = (xdt * dec_e).astype(BF16)
        for g in range(SSD_GROUPS):
            gc = slice(g * SSD_STATE, (g + 1) * SSD_STATE)
            hc = slice(g * GP, (g + 1) * GP)
            bg = b_ref[rows, gc]
            cg = c_ref[rows, gc]
            cb = _dot_nt(cg, bg)
            hst = h_ref[g]
            y_off = _dot(cg, hst.astype(BF16)) * ecs_e[:, hc]
            h_ref[g] = hst * ecs_e[L - 1:L, hc] + _dot_tn(bg, dx_b[:, hc])
            for p in range(SSD_HEADS_PER_GROUP // 2):
                h0 = g * SSD_HEADS_PER_GROUP + 2 * p
                pc = slice(h0 * SSD_HEAD_DIM, h0 * SSD_HEAD_DIM + P2)
                pair = xdt_b[:, pc]

                def lmat(hd):
                    seg = cs[:, hd:hd + 1] - cs_t[hd:hd + 1, :]
                    return (cb * jnp.exp2(jnp.where(causal, seg, -jnp.inf))).astype(BF16)

                lhs = jnp.concatenate([lmat(h0), lmat(h0 + 1)], axis=1)
                rhs = jnp.concatenate([jnp.where(left, pair, 0), jnp.where(left, 0, pair)], axis=0)
                y_ref[rows, pc] = (_dot(lhs, rhs) + y_off[:, p * P2:(p + 1) * P2]
                                   + xs[:, pc] * dskip_ref[:, pc])
        return carry

    lax.fori_loop(0, ts // L, chunk_body, 0, unroll=True)

    gw = SSD_D_INNER // SSD_GROUPS
    for r0 in range(0, ts, SSD_OUT_ROWS):
        rows = slice(r0, r0 + SSD_OUT_ROWS)
        gated = y_ref[rows, :] * _silu(z_ref[rows, :].astype(F32))
        parts = []
        for g in range(SSD_GROUPS):
            gg = gated[:, g * gw:(g + 1) * gw]
            parts.append(gg * lax.rsqrt(jnp.mean(gg * gg, axis=-1, keepdims=True) + EPS))
        yn = (jnp.concatenate(parts, axis=-1) * ng_ref[...]).astype(BF16)
        o_ref[rows, :] = x_ref[rows, :] + _dot(yn, wo_ref[...])


def _ssd_layer(x2, batch, seq, norm_g, in_w_layers, layer, conv_w, conv_b, dt_bias, a_log, d_skip, ssd_norm_g,
               out_w, casts):
    m = x2.shape[0]
    tm = ROW_TILE
    gn = SSD_GROUPS * SSD_STATE
    alog = jnp.pad(a_log, (0, V7X_LANES - SSD_HEADS))[None, :]

    def rows(width):
        return pl.BlockSpec((tm, width), lambda i: (i, 0))

    cast_in, cast_out, cast_shapes = _cast_specs(casts, m // tm)
    z, xs, bm, cm, dt, *cast = pl.pallas_call(
        functools.partial(_ssd_in_kernel, tiles_per_seq=seq // tm, n_casts=len(casts)),
        grid=(m // tm,),
        in_specs=[rows(D_MODEL), _const_spec((1, D_MODEL)),
                  pl.BlockSpec((None,) + in_w_layers.shape[1:], lambda i: (layer, 0, 0),
                               pipeline_mode=pl.Buffered(1)),
                  _const_spec(conv_w.shape), _const_spec((1, SSD_CONV_CH)), _const_spec((1, SSD_HEADS))] + cast_in,
        out_specs=[rows(SSD_D_INNER), rows(SSD_D_INNER), rows(gn), rows(gn), rows(V7X_LANES)] + cast_out,
        out_shape=[jax.ShapeDtypeStruct((m, SSD_D_INNER), BF16), jax.ShapeDtypeStruct((m, SSD_D_INNER), BF16),
                   jax.ShapeDtypeStruct((m, gn), BF16), jax.ShapeDtypeStruct((m, gn), BF16),
                   jax.ShapeDtypeStruct((m, V7X_LANES), F32)] + cast_shapes,
        scratch_shapes=[pltpu.VMEM((SSD_CONV_CH // V7X_LANES, V7X_SUBLANES + tm, V7X_LANES), F32)],
        compiler_params=_params(("arbitrary",)), name="ssd_in",
    )(x2, norm_g[None, :], in_w_layers, conv_w, conv_b[None, :], dt_bias[None, :], *[w for w, _ in casts])
    if out_w is None:
        out_w = cast.pop(0)

    expand = jnp.asarray(np.arange(V7X_LANES)[:, None] == (np.arange(SSD_D_INNER) // SSD_HEAD_DIM)[None, :], BF16)
    dskip = jnp.repeat(d_skip, SSD_HEAD_DIM)[None, :]
    ts = ROW_TILE
    spt = seq // ts

    def tile(width):
        return pl.BlockSpec((ts, width), lambda b, s: (b * spt + s, 0))

    out = pl.pallas_call(
        _ssd_core_kernel,
        grid=(batch, spt),
        in_specs=[tile(SSD_D_INNER), tile(gn), tile(gn), tile(V7X_LANES), tile(SSD_D_INNER), tile(D_MODEL),
                  _const_spec((1, V7X_LANES)), _const_spec((1, SSD_D_INNER)), _const_spec(expand.shape),
                  _const_spec((1, SSD_D_INNER)), _const_spec((SSD_D_INNER, D_MODEL))],
        out_specs=tile(D_MODEL),
        out_shape=jax.ShapeDtypeStruct((m, D_MODEL), F32),
        scratch_shapes=[pltpu.VMEM((SSD_GROUPS, SSD_STATE, SSD_HEADS_PER_GROUP * SSD_HEAD_DIM), F32),
                        pltpu.VMEM((ts, SSD_D_INNER), F32)],
        compiler_params=_params(("arbitrary", "arbitrary")), name="ssd_core",
    )(xs, bm, cm, dt, z, x2, alog, dskip, expand, ssd_norm_g[None, :], out_w)
    return out, cast


def _attn_in_kernel(x_ref, g_ref, wqk_ref, wvt_ref, c_ref, s1_ref, s2_ref, qg_ref, kg_ref, seg_ref,
                    q_ref, k_ref, vt_ref):
    t = ATTN_TILE
    hw = 2 * ATTN_HEAD_DIM
    half = ROPE_DIM // 2
    tm = x_ref.shape[0]
    h = _rmsnorm(x_ref[...], g_ref[...]).astype(BF16)
    cos_t, sin1_t, sin2_t = c_ref[...], s1_ref[...], s2_ref[...]
    q_scale = LOG2_E / math.sqrt(ATTN_HEAD_DIM)
    for c0, o_ref, gain_ref, scale in ((0, q_ref, qg_ref, q_scale), (D_MODEL, k_ref, kg_ref, None)):
        a = _dot(h, wqk_ref[:, c0:c0 + D_MODEL])
        for p0 in range(0, D_MODEL, 2 * hw):
            pair = a[:, p0:p0 + 2 * hw]
            ss = _dot((pair * pair).astype(BF16), seg_ref[...])
            inv = lax.rsqrt(ss * (1.0 / ATTN_HEAD_DIM) + EPS)
            for j in range(2):
                lanes = slice(j * hw, (j + 1) * hw)
                xn = pair[:, lanes] * inv[:, lanes] * gain_ref[...]
                y = (xn * cos_t + pltpu.roll(xn, hw - half, axis=1) * sin1_t
                     + pltpu.roll(xn, half, axis=1) * sin2_t)
                o_ref[:, p0 + j * hw:p0 + (j + 1) * hw] = (y if scale is None else y * scale).astype(BF16)
    vt = _dot_nt(wvt_ref[...], h).astype(BF16)
    ones = jnp.ones((ATTN_V_ROWS - ATTN_V_DIM, tm), BF16)
    pieces = []
    for g in range(ATTN_HEADS):
        pieces += [vt[g * ATTN_V_DIM:(g + 1) * ATTN_V_DIM], ones]
    vt = jnp.concatenate(pieces, axis=0)
    for j in range(tm // t):
        vt_ref[j] = vt[:, j * t:(j + 1) * t]


def _flash_kernel(q_ref, k_ref, vt_ref, x_ref, lq1_ref, lk1_ref, lq2_ref, lk2_ref, sg_ref, wo_ref,
                  o_ref, acc_ref, *, lambda_init):
    t = ATTN_TILE
    hw = 2 * ATTN_HEAD_DIM
    qi = pl.program_id(1)
    left_row = lax.broadcasted_iota(jnp.int32, (1, V7X_LANES), 1) < ATTN_HEAD_DIM
    heads = range(ATTN_HEADS)

    qa, qb = [], []
    for g in heads:
        qh = q_ref[:, g * hw:(g + 1) * hw]
        qa.append(jnp.where(left_row, qh, 0))
        qb.append(jnp.where(left_row, 0, qh))

    def kv_tile(kt, carry, masked=False):
        rows = pl.ds(pl.multiple_of(kt * t, t), t)
        scores = []
        for g in heads:
            ks = k_ref[rows, g * hw:(g + 1) * hw]
            scores.append((_dot_nt(ks, qa[g]), _dot_nt(ks, qb[g])))
        if masked:
            kc = lax.broadcasted_iota(jnp.int32, (t, t), 0) // CHUNK
            qc = lax.broadcasted_iota(jnp.int32, (t, t), 1) // CHUNK
            allowed = kc <= qc
            scores = [tuple(jnp.where(allowed, s, -jnp.inf) for s in pair) for pair in scores]
        probs, out = [], []
        for g in heads:
            new, alphas, ps = [], [], []
            for c, s in enumerate(scores[g]):
                m = jnp.max(s, axis=0, keepdims=True)
                if carry is not None:
                    m_old = carry[g][c]
                    m = jnp.maximum(m_old, m)
                    alphas.append(jnp.exp2(m_old - m))
                ps.append(jnp.exp2(s - m))
                new.append(m)
            probs.append((jnp.concatenate(ps, axis=1).astype(BF16), alphas))
            out.append(tuple(new))
        for g in heads:
            p, alphas = probs[g]
            pv = _dot(vt_ref[kt, g * ATTN_V_ROWS:(g + 1) * ATTN_V_ROWS, :], p)
            if carry is None:
                acc_ref[g] = pv
            else:
                acc_ref[g] = acc_ref[g] * jnp.concatenate(alphas, axis=1) + pv
        return tuple(out)

    lax.fori_loop(0, qi, kv_tile, kv_tile(qi, None, True))

    lam = (jnp.exp(jnp.sum(lq1_ref[...] * lk1_ref[...], axis=-1, keepdims=True))
           - jnp.exp(jnp.sum(lq2_ref[...] * lk2_ref[...], axis=-1, keepdims=True)) + lambda_init)
    outs = []
    for g in heads:
        acc = acc_ref[g]
        la = acc[ATTN_V_DIM:ATTN_V_DIM + 1, :t]
        lb = acc[ATTN_V_DIM:ATTN_V_DIM + 1, t:]
        ot = acc[:ATTN_V_DIM, :t] / la - lam * (acc[:ATTN_V_DIM, t:] / lb)
        ot = ot * lax.rsqrt(jnp.mean(ot * ot, axis=0, keepdims=True) + EPS)
        outs.append((ot.T * sg_ref[...] * (1.0 - lambda_init)).astype(BF16))
    o_ref[...] = x_ref[...] + _dot(jnp.concatenate(outs, axis=1), wo_ref[...])


def _attn_layer(x2, batch, seq, tables, norm_g, in_w, q_norm_g, k_norm_g, lq1, lk1, lq2, lk2,
                subln_g, out_w, lambda_init):
    m = x2.shape[0]
    tm = ROW_TILE
    tk = tq = ATTN_TILE
    qt = seq // tq
    hw = 2 * ATTN_HEAD_DIM
    wqk = in_w.astype(BF16)
    wvt = in_w[:, 2 * D_MODEL:].T.astype(BF16)
    cos_t, sin1_t, sin2_t = tables
    lane = np.arange(2 * hw)
    seg = jnp.asarray(lane[:, None] // ATTN_HEAD_DIM == lane[None, :] // ATTN_HEAD_DIM, BF16)
    rows_spec = pl.BlockSpec((tm, D_MODEL), lambda i: (i, 0))
    table_spec = pl.BlockSpec((tm, V7X_LANES), lambda i: (i, 0))
    qkv_shape = jax.ShapeDtypeStruct((m, D_MODEL), BF16)
    q, k, vt = pl.pallas_call(
        _attn_in_kernel,
        grid=(m // tm,),
        in_specs=[rows_spec, _const_spec((1, D_MODEL)), _const_spec(wqk.shape), _const_spec(wvt.shape),
                  table_spec, table_spec, table_spec, _const_spec((1, hw)), _const_spec((1, hw)),
                  _const_spec(seg.shape)],
        out_specs=[rows_spec, rows_spec,
                   pl.BlockSpec((tm // tk, ATTN_HEADS * ATTN_V_ROWS, tk), lambda i: (i, 0, 0))],
        out_shape=[qkv_shape, qkv_shape,
                   jax.ShapeDtypeStruct((m // tk, ATTN_HEADS * ATTN_V_ROWS, tk), BF16)],
        compiler_params=_params(("arbitrary",)), name="attn_in",
    )(x2, norm_g[None, :], wqk, wvt, cos_t, sin1_t, sin2_t, jnp.tile(q_norm_g, 2)[None, :],
      jnp.tile(k_norm_g, 2)[None, :], seg)

    vec64 = _const_spec((1, ATTN_HEAD_DIM))
    tile_spec = pl.BlockSpec((tq, D_MODEL), lambda b, qi: (b * qt + qi, 0))
    return pl.pallas_call(
        functools.partial(_flash_kernel, lambda_init=lambda_init),
        grid=(batch, qt),
        in_specs=[tile_spec,
                  pl.BlockSpec((seq, D_MODEL), lambda b, qi: (b, 0)),
                  pl.BlockSpec((seq // tk, ATTN_HEADS * ATTN_V_ROWS, tk), lambda b, qi: (b, 0, 0)),
                  tile_spec, vec64, vec64, vec64, vec64, _const_spec((1, ATTN_V_DIM)),
                  _const_spec((D_MODEL, D_MODEL))],
        out_specs=tile_spec,
        out_shape=jax.ShapeDtypeStruct((m, D_MODEL), F32),
        scratch_shapes=[pltpu.VMEM((ATTN_HEADS, ATTN_V_ROWS, 2 * tq), F32)],
        compiler_params=_params(("arbitrary", "arbitrary")), name="flash",
    )(q, k, vt, x2, lq1[None, :], lk1[None, :], lq2[None, :], lk2[None, :], subln_g[None, :],
      out_w.astype(BF16))


def _ffn_kernel(*refs, tiles_per_seq, n_casts):
    x_ref, g_ref, wup_ref, cw_ref, cb_ref, wd_ref = refs[:6]
    cast_in = refs[6:6 + n_casts]
    o_ref = refs[6 + n_casts]
    cast_out = refs[7 + n_casts:7 + 2 * n_casts]
    buf_g_ref, buf_u_ref, act_ref = refs[7 + 2 * n_casts:]
    first = pl.program_id(0) % tiles_per_seq == 0
    _conv_history_reset(buf_g_ref, first)
    _conv_history_reset(buf_u_ref, first)
    x = x_ref[...]
    h = _rmsnorm(x, g_ref[...]).astype(BF16)
    for c0 in range(0, D_FF, FFN_COLS):
        _conv_store(buf_g_ref, c0, _dot(h, wup_ref[:, c0:c0 + FFN_COLS]))
        _conv_store(buf_u_ref, c0, _dot(h, wup_ref[:, D_FF + c0:D_FF + c0 + FFN_COLS]))
        for l0 in range(c0, c0 + FFN_COLS, V7X_LANES):
            gl = slice(l0, l0 + V7X_LANES)
            ul = slice(D_FF + l0, D_FF + l0 + V7X_LANES)
            yg = _causal_conv(buf_g_ref, l0 // V7X_LANES, cw_ref[:, gl], cb_ref[:, gl])
            yu = _causal_conv(buf_u_ref, l0 // V7X_LANES, cw_ref[:, ul], cb_ref[:, ul])
            act_ref[:, gl] = (_silu(yg) * yu).astype(BF16)
    _conv_history_carry(buf_g_ref)
    _conv_history_carry(buf_u_ref)
    _cast_blocks(cast_in, cast_out)
    o_ref[...] = x + _dot(act_ref[...], wd_ref[...])


def _ffn_layer(x2, seq, norm_g, up_w, conv_w, conv_b, down_w, casts):
    m = x2.shape[0]
    tm = ROW_TILE
    steps = m // tm
    row = pl.BlockSpec((tm, D_MODEL), lambda i: (i, 0))
    cast_in, cast_out, cast_shapes = _cast_specs(casts, steps)
    outs = pl.pallas_call(
        functools.partial(_ffn_kernel, tiles_per_seq=seq // tm, n_casts=len(casts)),
        grid=(steps,),
        in_specs=[row, _const_spec((1, D_MODEL)), _const_spec((D_MODEL, 2 * D_FF)), _const_spec((FFN_CONV, 2 * D_FF)),
                  _const_spec((1, 2 * D_FF)), _const_spec((D_FF, D_MODEL))] + cast_in,
        out_specs=[row] + cast_out,
        out_shape=[jax.ShapeDtypeStruct((m, D_MODEL), F32)] + cast_shapes,
        scratch_shapes=[pltpu.VMEM((D_FF // V7X_LANES, V7X_SUBLANES + tm, V7X_LANES), F32),
                        pltpu.VMEM((D_FF // V7X_LANES, V7X_SUBLANES + tm, V7X_LANES), F32),
                        pltpu.VMEM((tm, D_FF), BF16)],
        compiler_params=_params(("arbitrary",)), name="ffn",
    )(x2, norm_g[None, :], up_w, conv_w, conv_b[None, :], down_w, *[w for w, _ in casts])
    return outs[0], outs[1:]


def kernel(x, positions, norm_mix_g, norm_ffn_g, ssd_in_w, ssd_conv_w, ssd_conv_b, ssd_dt_bias, ssd_a_log, ssd_d, ssd_norm_g, ssd_out_w, attn_in_w, attn_q_norm_g, attn_k_norm_g, attn_lq1, attn_lk1, attn_lq2, attn_lk2, attn_subln_g, attn_out_w, ffn_up_w, ffn_conv_w, ffn_conv_b, ffn_down_w):
    batch, seq, d = x.shape
    depth = norm_mix_g.shape[0]
    assert d == D_MODEL and seq % ROW_TILE == 0 and ROW_TILE % ATTN_TILE == 0 and ATTN_TILE % CHUNK == 0
    assert ssd_conv_w.shape[1] - 1 <= V7X_SUBLANES and ffn_conv_w.shape[1] - 1 <= V7X_SUBLANES
    x2 = x.reshape(batch * seq, d)
    tables = _rope_tables(positions)

    def mixer_weights(i):
        j = i // N_MIXERS
        return [(ssd_out_w, j)] if i % N_MIXERS == 0 else [(attn_in_w, j), (attn_out_w, j)]

    ssd_in_bf = ssd_in_w.astype(BF16)
    mix_w, ffn_w = [None], None
    for i in range(depth):
        j = i // N_MIXERS
        if i % N_MIXERS == 0:
            first_casts = [(ssd_out_w, 0), (ffn_up_w, 0), (ffn_down_w, 0)] if i == 0 else []
            x2, cast = _ssd_layer(x2, batch, seq, norm_mix_g[i], ssd_in_bf, j, ssd_conv_w[j], ssd_conv_b[j],
                                  ssd_dt_bias[j], ssd_a_log[j], ssd_d[j], ssd_norm_g[j], mix_w[0], first_casts)
            if i == 0:
                ffn_w = cast
        else:
            lambda_init = 0.8 - 0.6 * math.exp(-0.3 * i)
            x2 = _attn_layer(x2, batch, seq, tables, norm_mix_g[i], mix_w[0], attn_q_norm_g[j],
                             attn_k_norm_g[j], attn_lq1[j], attn_lk1[j], attn_lq2[j], attn_lk2[j],
                             attn_subln_g[j], mix_w[1], lambda_init)
        casts = mixer_weights(i + 1) + [(ffn_up_w, i + 1), (ffn_down_w, i + 1)] if i + 1 < depth else []
        x2, cast = _ffn_layer(x2, seq, norm_ffn_g[i], ffn_w[0], ffn_conv_w[i], ffn_conv_b[i], ffn_w[1], casts)
        mix_w, ffn_w = cast[:-2], cast[-2:]
    return x2.reshape(batch, seq, d)
```

```python
import functools
import math

import jax
import jax.numpy as jnp
import numpy as np
from jax import lax
from jax.experimental import pallas as pl
from jax.experimental.pallas import tpu as pltpu

F32 = jnp.float32
BF16 = jnp.bfloat16

V7X_LANES = 128
V7X_SUBLANES = 8
V7X_BF16_SUBLANES = 16
V7X_VMEM_LIMIT_BYTES = 56 * 1024 * 1024

D_MODEL = 1024
CHUNK = 64
EPS = 1e-6
SSD_D_INNER = 2048
SSD_HEAD_DIM = 64
SSD_HEADS = 32
SSD_GROUPS = 4
SSD_HEADS_PER_GROUP = 8
SSD_STATE = 128
SSD_CONV_CH = SSD_D_INNER + 2 * SSD_GROUPS * SSD_STATE
ATTN_HEADS = 8
ATTN_HEAD_DIM = 64
ATTN_V_DIM = 128
ATTN_V_ROWS = ATTN_V_DIM + V7X_BF16_SUBLANES
LOG2_E = math.log2(math.e)
ROPE_THETA = 500000.0
ROPE_DIM = 16
D_FF = 2816
FFN_CONV = 3
N_MIXERS = 2

ROW_TILE = 512
SSD_SCAN_CHUNK = 128
SSD_CONV_COLS = 512
SSD_OUT_ROWS = 256
FFN_COLS = 256
ATTN_TILE = 256


def _dot(a, b):
    return jnp.dot(a, b, preferred_element_type=F32)


def _dot_nt(a, b):
    return lax.dot_general(a, b, (((1,), (1,)), ((), ())), preferred_element_type=F32)


def _dot_tn(a, b):
    return lax.dot_general(a, b, (((0,), (0,)), ((), ())), preferred_element_type=F32)


def _silu(x):
    h = 0.5 * x
    return h + h * jnp.tanh(h)


def _rmsnorm(x, g):
    return x * lax.rsqrt(jnp.mean(x * x, axis=-1, keepdims=True) + EPS) * g


def _conv_store(buf_ref, c0, a):
    for j in range(a.shape[1] // V7X_LANES):
        buf_ref[c0 // V7X_LANES + j, V7X_SUBLANES:, :] = a[:, j * V7X_LANES:(j + 1) * V7X_LANES]


def _causal_conv(buf_ref, slab, w, b):
    kw = w.shape[0]
    tm = buf_ref.shape[1] - V7X_SUBLANES
    y = b + w[kw - 1:kw] * buf_ref[slab, V7X_SUBLANES:, :]
    for j in range(1, kw):
        y = y + w[kw - 1 - j:kw - j] * buf_ref[slab, V7X_SUBLANES - j:V7X_SUBLANES - j + tm, :]
    return y


def _conv_history_reset(buf_ref, first):
    @pl.when(first)
    def _():
        buf_ref[:, :V7X_SUBLANES, :] = jnp.zeros((buf_ref.shape[0], V7X_SUBLANES, V7X_LANES), buf_ref.dtype)


def _conv_history_carry(buf_ref):
    tm = buf_ref.shape[1] - V7X_SUBLANES
    buf_ref[:, :V7X_SUBLANES, :] = buf_ref[:, tm:, :]


def _const_spec(shape):
    nd = len(shape)
    return pl.BlockSpec(shape, lambda *_: (0,) * nd, pipeline_mode=pl.Buffered(1))


def _params(semantics):
    return pltpu.CompilerParams(dimension_semantics=semantics,
                                vmem_limit_bytes=V7X_VMEM_LIMIT_BYTES)


def _cast_blocks(src_refs, dst_refs):
    for src, dst in zip(src_refs, dst_refs):
        dst[...] = src[...].astype(BF16)


def _cast_blocking(rows, cols, n_steps):
    for ncb in (1, 2, 4, 8):
        nrb = n_steps // ncb
        if (rows % nrb == 0 and (rows // nrb) % V7X_BF16_SUBLANES == 0 and cols % ncb == 0
                and (ncb == 1 or (cols // ncb) % V7X_LANES == 0)):
            return rows // nrb, cols // ncb, ncb
    raise ValueError(f"no {n_steps}-block tiling for a ({rows}, {cols}) weight")


def _cast_specs(casts, n_steps):
    in_specs, out_specs, out_shapes = [], [], []
    for w, layer in casts:
        _, r, c = w.shape
        br, bc, ncb = _cast_blocking(r, c, n_steps)
        in_specs.append(pl.BlockSpec((None, br, bc), lambda i, layer=layer, ncb=ncb: (layer, i // ncb, i % ncb)))
        out_specs.append(pl.BlockSpec((br, bc), lambda i, ncb=ncb: (i // ncb, i % ncb)))
        out_shapes.append(jax.ShapeDtypeStruct((r, c), BF16))
    return in_specs, out_specs, out_shapes


def _rope_kernel(pos_ref, c_ref, s1_ref, s2_ref):
    seq = pos_ref.shape[1]
    hs = seq // 2
    half = ROPE_DIM // 2
    pos = jnp.broadcast_to(pos_ref[...].astype(F32), (V7X_LANES, seq)).T
    lane = lax.broadcasted_iota(jnp.int32, (1, V7X_LANES), 1)
    first = lane < ATTN_HEAD_DIM
    d = lane % ATTN_HEAD_DIM
    rotary = d < ROPE_DIM
    inv_freq = ROPE_THETA ** (-(2 * (d % half)).astype(F32) / ROPE_DIM)
    ang = jnp.where(first, pos[:hs], pos[hs:]) * jnp.where(rotary, inv_freq, 0.0)
    m1 = jnp.where(d < half, -1.0, 0.0)
    m2 = jnp.where(rotary & (d >= half), 1.0, 0.0)
    for packed, refs in ((jnp.cos(ang), (c_ref,)), (jnp.sin(ang), (s1_ref, s2_ref))):
        swapped = pltpu.roll(packed, ATTN_HEAD_DIM, axis=1)
        lo = jnp.where(first, packed, swapped)
        hi = jnp.where(first, swapped, packed)
        for ref, sign in zip(refs, (m1, m2) if len(refs) == 2 else (None,)):
            ref[:hs, :] = lo if sign is None else lo * sign
            ref[hs:, :] = hi if sign is None else hi * sign


def _rope_tables(positions):
    b, s = positions.shape
    row = pl.BlockSpec((s, V7X_LANES), lambda i: (i, 0))
    out = jax.ShapeDtypeStruct((b * s, V7X_LANES), F32)
    return pl.pallas_call(
        _rope_kernel, grid=(b,), in_specs=[pl.BlockSpec((None, 1, s), lambda i: (i, 0, 0))],
        out_specs=[row, row, row], out_shape=[out, out, out],
        compiler_params=_params(("arbitrary",)), name="rope_tables",
    )(positions.reshape(b, 1, s))


def _ssd_in_kernel(*refs, tiles_per_seq, n_casts):
    x_ref, g_ref, w_ref, cw_ref, cb_ref, dtb_ref = refs[:6]
    cast_in = refs[6:6 + n_casts]
    z_ref, xs_ref, b_ref, c_ref, dt_ref = refs[6 + n_casts:11 + n_casts]
    cast_out = refs[11 + n_casts:11 + 2 * n_casts]
    buf_ref = refs[11 + 2 * n_casts]
    _cast_blocks(cast_in, cast_out)
    _conv_history_reset(buf_ref, pl.program_id(0) % tiles_per_seq == 0)
    h = _rmsnorm(x_ref[...], g_ref[...]).astype(BF16)
    gn = SSD_GROUPS * SSD_STATE
    n_chunks = SSD_CONV_CH // SSD_CONV_COLS
    zw = SSD_CONV_COLS
    xbc0 = SSD_D_INNER
    dt0 = SSD_D_INNER + SSD_CONV_CH

    def project_z(c):
        if (c + 1) * zw <= SSD_D_INNER:
            z_ref[:, c * zw:(c + 1) * zw] = _dot(h, w_ref[:, c * zw:(c + 1) * zw]).astype(BF16)
        elif c * zw == SSD_D_INNER:
            dt = _dot(h, w_ref[:, dt0:]) + dtb_ref[...]
            dt_ref[...] = jnp.zeros(dt_ref.shape, F32)
            dt_ref[:, :SSD_HEADS] = jnp.maximum(dt, 0.0) + jnp.log1p(jnp.exp(-jnp.abs(dt)))

    _conv_store(buf_ref, 0, _dot(h, w_ref[:, xbc0:xbc0 + SSD_CONV_COLS]))
    for c in range(n_chunks):
        c0 = c * SSD_CONV_COLS
        if c + 1 < n_chunks:
            _conv_store(buf_ref, c0 + SSD_CONV_COLS,
                        _dot(h, w_ref[:, xbc0 + c0 + SSD_CONV_COLS:xbc0 + c0 + 2 * SSD_CONV_COLS]))
        project_z(c)
        for l0 in range(c0, c0 + SSD_CONV_COLS, V7X_LANES):
            lanes = slice(l0, l0 + V7X_LANES)
            y = _silu(_causal_conv(buf_ref, l0 // V7X_LANES, cw_ref[:, lanes], cb_ref[:, lanes])).astype(BF16)
            if l0 < SSD_D_INNER:
                xs_ref[:, lanes] = y
            elif l0 < SSD_D_INNER + gn:
                b_ref[:, l0 - SSD_D_INNER:l0 - SSD_D_INNER + V7X_LANES] = y
            else:
                c_ref[:, l0 - SSD_D_INNER - gn:l0 - SSD_D_INNER - gn + V7X_LANES] = y
    _conv_history_carry(buf_ref)


def _ssd_core_kernel(xs_ref, b_ref, c_ref, dt_ref, z_ref, x_ref, alog_ref, dskip_ref, e_ref,
                     ng_ref, wo_ref, o_ref, h_ref, y_ref):
    ts = xs_ref.shape[0]
    L = SSD_SCAN_CHUNK
    P2 = 2 * SSD_HEAD_DIM
    GP = SSD_HEADS_PER_GROUP * SSD_HEAD_DIM

    @pl.when(pl.program_id(1) == 0)
    def _():
        h_ref[...] = jnp.zeros_like(h_ref)

    row = lax.broadcasted_iota(jnp.int32, (L, L), 0)
    col = lax.broadcasted_iota(jnp.int32, (L, L), 1)
    causal = row >= col
    tril = jnp.where(causal, 1.0, 0.0).astype(BF16)
    left = lax.broadcasted_iota(jnp.int32, (L, P2), 1) < SSD_HEAD_DIM
    a_row = -jnp.exp(alog_ref[...])

    def chunk_body(ci, carry):
        rows = pl.ds(pl.multiple_of(ci * L, L), L)
        dt = dt_ref[rows, :]
        da = dt * a_row
        hi = da.astype(BF16)
        r1 = da - hi.astype(F32)
        mid = r1.astype(BF16)
        lo = (r1 - mid.astype(F32)).astype(BF16)
        cs = (_dot(tril, hi) + _dot(tril, mid) + _dot(tril, lo)) * LOG2_E
        cs_t = cs.T
        ecs = jnp.exp2(cs)
        dec = jnp.exp2(cs[L - 1:L, :] - cs)
        stacked = jnp.concatenate([dt, dec, ecs], axis=0).astype(BF16)
        wide = _dot(stacked, e_ref[...])
        dt_e, dec_e, ecs_e = wide[:L], wide[L:2 * L], wide[2 * L:]
        xs = xs_ref[rows, :].astype(F32)
        xdt = xs * dt_e
        xdt_b = xdt.astype(BF16)
        dx_b = (xdt * dec_e).astype(BF16)
        for g in range(SSD_GROUPS):
            gc = slice(g * SSD_STATE, (g + 1) * SSD_STATE)
            hc = slice(g * GP, (g + 1) * GP)
            bg = b_ref[rows, gc]
            cg = c_ref[rows, gc]
            cb = _dot_nt(cg, bg)
            hst = h_ref[g]
            y_off = _dot(cg, hst.astype(BF16)) * ecs_e[:, hc]
            h_ref[g] = hst * ecs_e[L - 1:L, hc] + _dot_tn(bg, dx_b[:, hc])
            for p in range(SSD_HEADS_PER_GROUP // 2):
                h0 = g * SSD_HEADS_PER_GROUP + 2 * p
                pc = slice(h0 * SSD_HEAD_DIM, h0 * SSD_HEAD_DIM + P2)
                pair = xdt_b[:, pc]

                def lmat(hd):
                    seg = cs[:, hd:hd + 1] - cs_t[hd:hd + 1, :]
                    return (cb * jnp.exp2(jnp.where(causal, seg, -jnp.inf))).astype(BF16)

                lhs = jnp.concatenate([lmat(h0), lmat(h0 + 1)], axis=1)
                rhs = jnp.concatenate([jnp.where(left, pair, 0), jnp.where(left, 0, pair)], axis=0)
                y_ref[rows, pc] = (_dot(lhs, rhs) + y_off[:, p * P2:(p + 1) * P2]
                                   + xs[:, pc] * dskip_ref[:, pc])
        return carry

    lax.fori_loop(0, ts // L, chunk_body, 0, unroll=True)

    gw = SSD_D_INNER // SSD_GROUPS
    for r0 in range(0, ts, SSD_OUT_ROWS):
        rows = slice(r0, r0 + SSD_OUT_ROWS)
        gated = y_ref[rows, :] * _silu(z_ref[rows, :].astype(F32))
        parts = []
        for g in range(SSD_GROUPS):
            gg = gated[:, g * gw:(g + 1) * gw]
            parts.append(gg * lax.rsqrt(jnp.mean(gg * gg, axis=-1, keepdims=True) + EPS))
        yn = (jnp.concatenate(parts, axis=-1) * ng_ref[...]).astype(BF16)
        o_ref[rows, :] = x_ref[rows, :] + _dot(yn, wo_ref[...])


def _ssd_layer(x2, batch, seq, norm_g, in_w_layers, layer, conv_w, conv_b, dt_bias, a_log, d_skip, ssd_norm_g,
               out_w, casts):
    m = x2.shape[0]
    tm = ROW_TILE
    gn = SSD_GROUPS * SSD_STATE
    alog = jnp.pad(a_log, (0, V7X_LANES - SSD_HEADS))[None, :]

    def rows(width):
        return pl.BlockSpec((tm, width), lambda i: (i, 0))

    cast_in, cast_out, cast_shapes = _cast_specs(casts, m // tm)
    z, xs, bm, cm, dt, *cast = pl.pallas_call(
        functools.partial(_ssd_in_kernel, tiles_per_seq=seq // tm, n_casts=len(casts)),
        grid=(m // tm,),
        in_specs=[rows(D_MODEL), _const_spec((1, D_MODEL)),
                  pl.BlockSpec((None,) + in_w_layers.shape[1:], lambda i: (layer, 0, 0),
                               pipeline_mode=pl.Buffered(1)),
                  _const_spec(conv_w.shape), _const_spec((1, SSD_CONV_CH)), _const_spec((1, SSD_HEADS))] + cast_in,
        out_specs=[rows(SSD_D_INNER), rows(SSD_D_INNER), rows(gn), rows(gn), rows(V7X_LANES)] + cast_out,
        out_shape=[jax.ShapeDtypeStruct((m, SSD_D_INNER), BF16), jax.ShapeDtypeStruct((m, SSD_D_INNER), BF16),
                   jax.ShapeDtypeStruct((m, gn), BF16), jax.ShapeDtypeStruct((m, gn), BF16),
                   jax.ShapeDtypeStruct((m, V7X_LANES), F32)] + cast_shapes,
        scratch_shapes=[pltpu.VMEM((SSD_CONV_CH // V7X_LANES, V7X_SUBLANES + tm, V7X_LANES), F32)],
        compiler_params=_params(("arbitrary",)), name="ssd_in",
    )(x2, norm_g[None, :], in_w_layers, conv_w, conv_b[None, :], dt_bias[None, :], *[w for w, _ in casts])
    if out_w is None:
        out_w = cast.pop(0)

    expand = jnp.asarray(np.arange(V7X_LANES)[:, None] == (np.arange(SSD_D_INNER) // SSD_HEAD_DIM)[None, :], BF16)
    dskip = jnp.repeat(d_skip, SSD_HEAD_DIM)[None, :]
    ts = ROW_TILE
    spt = seq // ts

    def tile(width):
        return pl.BlockSpec((ts, width), lambda b, s: (b * spt + s, 0))

    out = pl.pallas_call(
        _ssd_core_kernel,
        grid=(batch, spt),
        in_specs=[tile(SSD_D_INNER), tile(gn), tile(gn), tile(V7X_LANES), tile(SSD_D_INNER), tile(D_MODEL),
                  _const_spec((1, V7X_LANES)), _const_spec((1, SSD_D_INNER)), _const_spec(expand.shape),
                  _const_spec((1, SSD_D_INNER)), _const_spec((SSD_D_INNER, D_MODEL))],
        out_specs=tile(D_MODEL),
        out_shape=jax.ShapeDtypeStruct((m, D_MODEL), F32),
        scratch_shapes=[pltpu.VMEM((SSD_GROUPS, SSD_STATE, SSD_HEADS_PER_GROUP * SSD_HEAD_DIM), F32),
                        pltpu.VMEM((ts, SSD_D_INNER), F32)],
        compiler_params=_params(("arbitrary", "arbitrary")), name="ssd_core",
    )(xs, bm, cm, dt, z, x2, alog, dskip, expand, ssd_norm_g[None, :], out_w)
    return out, cast


def _attn_in_kernel(x_ref, g_ref, wqk_ref, wvt_ref, c_ref, s1_ref, s2_ref, qg_ref, kg_ref, seg_ref,
                    q_ref, k_ref, vt_ref):
    t = ATTN_TILE
    hw = 2 * ATTN_HEAD_DIM
    half = ROPE_DIM // 2
    tm = x_ref.shape[0]
    h = _rmsnorm(x_ref[...], g_ref[...]).astype(BF16)
    cos_t, sin1_t, sin2_t = c_ref[...], s1_ref[...], s2_ref[...]
    q_scale = LOG2_E / math.sqrt(ATTN_HEAD_DIM)
    for c0, o_ref, gain_ref, scale in ((0, q_ref, qg_ref, q_scale), (D_MODEL, k_ref, kg_ref, None)):
        a = _dot(h, wqk_ref[:, c0:c0 + D_MODEL])
        for p0 in range(0, D_MODEL, 2 * hw):
            pair = a[:, p0:p0 + 2 * hw]
            ss = _dot((pair * pair).astype(BF16), seg_ref[...])
            inv = lax.rsqrt(ss * (1.0 / ATTN_HEAD_DIM) + EPS)
            for j in range(2):
                lanes = slice(j * hw, (j + 1) * hw)
                xn = pair[:, lanes] * inv[:, lanes] * gain_ref[...]
                y = (xn * cos_t + pltpu.roll(xn, hw - half, axis=1) * sin1_t
                     + pltpu.roll(xn, half, axis=1) * sin2_t)
                o_ref[:, p0 + j * hw:p0 + (j + 1) * hw] = (y if scale is None else y * scale).astype(BF16)
    vt = _dot_nt(wvt_ref[...], h).astype(BF16)
    ones = jnp.ones((ATTN_V_ROWS - ATTN_V_DIM, tm), BF16)
    pieces = []
    for g in range(ATTN_HEADS):
        pieces += [vt[g * ATTN_V_DIM:(g + 1) * ATTN_V_DIM], ones]
    vt = jnp.concatenate(pieces, axis=0)
    for j in range(tm // t):
        vt_ref[j] = vt[:, j * t:(j + 1) * t]


def _flash_kernel(q_ref, k_ref, vt_ref, x_ref, lq1_ref, lk1_ref, lq2_ref, lk2_ref, sg_ref, wo_ref,
                  o_ref, acc_ref, *, lambda_init):
    t = ATTN_TILE
    hw = 2 * ATTN_HEAD_DIM
    qi = pl.program_id(1)
    left_row = lax.broadcasted_iota(jnp.int32, (1, V7X_LANES), 1) < ATTN_HEAD_DIM
    heads = range(ATTN_HEADS)

    qa, qb = [], []
    for g in heads:
        qh = q_ref[:, g * hw:(g + 1) * hw]
        qa.append(jnp.where(left_row, qh, 0))
        qb.append(jnp.where(left_row, 0, qh))

    def kv_tile(kt, carry, masked=False):
        rows = pl.ds(pl.multiple_of(kt * t, t), t)
        scores = []
        for g in heads:
            ks = k_ref[rows, g * hw:(g + 1) * hw]
            scores.append((_dot_nt(ks, qa[g]), _dot_nt(ks, qb[g])))
        if masked:
            kc = lax.broadcasted_iota(jnp.int32, (t, t), 0) // CHUNK
            qc = lax.broadcasted_iota(jnp.int32, (t, t), 1) // CHUNK
            allowed = kc <= qc
            scores = [tuple(jnp.where(allowed, s, -jnp.inf) for s in pair) for pair in scores]
        probs, out = [], []
        for g in heads:
            new, alphas, ps = [], [], []
            for c, s in enumerate(scores[g]):
                m = jnp.max(s, axis=0, keepdims=True)
                if carry is not None:
                    m_old = carry[g][c]
                    m = jnp.maximum(m_old, m)
                    alphas.append(jnp.exp2(m_old - m))
                ps.append(jnp.exp2(s - m))
                new.append(m)
            probs.append((jnp.concatenate(ps, axis=1).astype(BF16), alphas))
            out.append(tuple(new))
        for g in heads:
            p, alphas = probs[g]
            pv = _dot(vt_ref[kt, g * ATTN_V_ROWS:(g + 1) * ATTN_V_ROWS, :], p)
            if carry is None:
                acc_ref[g] = pv
            else:
                acc_ref[g] = acc_ref[g] * jnp.concatenate(alphas, axis=1) + pv
        return tuple(out)

    lax.fori_loop(0, qi, kv_tile, kv_tile(qi, None, True))

    lam = (jnp.exp(jnp.sum(lq1_ref[...] * lk1_ref[...], axis=-1, keepdims=True))
           - jnp.exp(jnp.sum(lq2_ref[...] * lk2_ref[...], axis=-1, keepdims=True)) + lambda_init)
    outs = []
    for g in heads:
        acc = acc_ref[g]
        la = acc[ATTN_V_DIM:ATTN_V_DIM + 1, :t]
        lb = acc[ATTN_V_DIM:ATTN_V_DIM + 1, t:]
        ot = acc[:ATTN_V_DIM, :t] / la - lam * (acc[:ATTN_V_DIM, t:] / lb)
        ot = ot * lax.rsqrt(jnp.mean(ot * ot, axis=0, keepdims=True) + EPS)
        outs.append((ot.T * sg_ref[...] * (1.0 - lambda_init)).astype(BF16))
    o_ref[...] = x_ref[...] + _dot(jnp.concatenate(outs, axis=1), wo_ref[...])


def _attn_layer(x2, batch, seq, tables, norm_g, in_w, q_norm_g, k_norm_g, lq1, lk1, lq2, lk2,
                subln_g, out_w, lambda_init):
    m = x2.shape[0]
    tm = ROW_TILE
    tk = tq = ATTN_TILE
    qt = seq // tq
    hw = 2 * ATTN_HEAD_DIM
    wqk = in_w.astype(BF16)
    wvt = in_w[:, 2 * D_MODEL:].T.astype(BF16)
    cos_t, sin1_t, sin2_t = tables
    lane = np.arange(2 * hw)
    seg = jnp.asarray(lane[:, None] // ATTN_HEAD_DIM == lane[None, :] // ATTN_HEAD_DIM, BF16)
    rows_spec = pl.BlockSpec((tm, D_MODEL), lambda i: (i, 0))
    table_spec = pl.BlockSpec((tm, V7X_LANES), lambda i: (i, 0))
    qkv_shape = jax.ShapeDtypeStruct((m, D_MODEL), BF16)
    q, k, vt = pl.pallas_call(
        _attn_in_kernel,
        grid=(m // tm,),
        in_specs=[rows_spec, _const_spec((1, D_MODEL)), _const_spec(wqk.shape), _const_spec(wvt.shape),
                  table_spec, table_spec, table_spec, _const_spec((1, hw)), _const_spec((1, hw)),
                  _const_spec(seg.shape)],
        out_specs=[rows_spec, rows_spec,
                   pl.BlockSpec((tm // tk, ATTN_HEADS * ATTN_V_ROWS, tk), lambda i: (i, 0, 0))],
        out_shape=[qkv_shape, qkv_shape,
                   jax.ShapeDtypeStruct((m // tk, ATTN_HEADS * ATTN_V_ROWS, tk), BF16)],
        compiler_params=_params(("arbitrary",)), name="attn_in",
    )(x2, norm_g[None, :], wqk, wvt, cos_t, sin1_t, sin2_t, jnp.tile(q_norm_g, 2)[None, :],
      jnp.tile(k_norm_g, 2)[None, :], seg)

    vec64 = _const_spec((1, ATTN_HEAD_DIM))
    tile_spec = pl.BlockSpec((tq, D_MODEL), lambda b, qi: (b * qt + qi, 0))
    return pl.pallas_call(
        functools.partial(_flash_kernel, lambda_init=lambda_init),
        grid=(batch, qt),
        in_specs=[tile_spec,
                  pl.BlockSpec((seq, D_MODEL), lambda b, qi: (b, 0)),
                  pl.BlockSpec((seq // tk, ATTN_HEADS * ATTN_V_ROWS, tk), lambda b, qi: (b, 0, 0)),
                  tile_spec, vec64, vec64, vec64, vec64, _const_spec((1, ATTN_V_DIM)),
                  _const_spec((D_MODEL, D_MODEL))],
        out_specs=tile_spec,
        out_shape=jax.ShapeDtypeStruct((m, D_MODEL), F32),
        scratch_shapes=[pltpu.VMEM((ATTN_HEADS, ATTN_V_ROWS, 2 * tq), F32)],
        compiler_params=_params(("arbitrary", "arbitrary")), name="flash",
    )(q, k, vt, x2, lq1[None, :], lk1[None, :], lq2[None, :], lk2[None, :], subln_g[None, :],
      out_w.astype(BF16))


def _ffn_kernel(*refs, tiles_per_seq, n_casts):
    x_ref, g_ref, wup_ref, cw_ref, cb_ref, wd_ref = refs[:6]
    cast_in = refs[6:6 + n_casts]
    o_ref = refs[6 + n_casts]
    cast_out = refs[7 + n_casts:7 + 2 * n_casts]
    buf_g_ref, buf_u_ref, act_ref = refs[7 + 2 * n_casts:]
    first = pl.program_id(0) % tiles_per_seq == 0
    _conv_history_reset(buf_g_ref, first)
    _conv_history_reset(buf_u_ref, first)
    x = x_ref[...]
    h = _rmsnorm(x, g_ref[...]).astype(BF16)
    for c0 in range(0, D_FF, FFN_COLS):
        _conv_store(buf_g_ref, c0, _dot(h, wup_ref[:, c0:c0 + FFN_COLS]))
        _conv_store(buf_u_ref, c0, _dot(h, wup_ref[:, D_FF + c0:D_FF + c0 + FFN_COLS]))
        for l0 in range(c0, c0 + FFN_COLS, V7X_LANES):
            gl = slice(l0, l0 + V7X_LANES)
            ul = slice(D_FF + l0, D_FF + l0 + V7X_LANES)
            yg = _causal_conv(buf_g_ref, l0 // V7X_LANES, cw_ref[:, gl], cb_ref[:, gl])
            yu = _causal_conv(buf_u_ref, l0 // V7X_LANES, cw_ref[:, ul], cb_ref[:, ul])
            act_ref[:, gl] = (_silu(yg) * yu).astype(BF16)
    _conv_history_carry(buf_g_ref)
    _conv_history_carry(buf_u_ref)
    _cast_blocks(cast_in, cast_out)
    o_ref[...] = x + _dot(act_ref[...], wd_ref[...])


def _ffn_layer(x2, seq, norm_g, up_w, conv_w, conv_b, down_w, casts):
    m = x2.shape[0]
    tm = ROW_TILE
    steps = m // tm
    row = pl.BlockSpec((tm, D_MODEL), lambda i: (i, 0))
    cast_in, cast_out, cast_shapes = _cast_specs(casts, steps)
    outs = pl.pallas_call(
        functools.partial(_ffn_kernel, tiles_per_seq=seq // tm, n_casts=len(casts)),
        grid=(steps,),
        in_specs=[row, _const_spec((1, D_MODEL)), _const_spec((D_MODEL, 2 * D_FF)), _const_spec((FFN_CONV, 2 * D_FF)),
                  _const_spec((1, 2 * D_FF)), _const_spec((D_FF, D_MODEL))] + cast_in,
        out_specs=[row] + cast_out,
        out_shape=[jax.ShapeDtypeStruct((m, D_MODEL), F32)] + cast_shapes,
        scratch_shapes=[pltpu.VMEM((D_FF // V7X_LANES, V7X_SUBLANES + tm, V7X_LANES), F32),
                        pltpu.VMEM((D_FF // V7X_LANES, V7X_SUBLANES + tm, V7X_LANES), F32),
                        pltpu.VMEM((tm, D_FF), BF16)],
        compiler_params=_params(("arbitrary",)), name="ffn",
    )(x2, norm_g[None, :], up_w, conv_w, conv_b[None, :], down_w, *[w for w, _ in casts])
    return outs[0], outs[1:]


def kernel(x, positions, norm_mix_g, norm_ffn_g, ssd_in_w, ssd_conv_w, ssd_conv_b, ssd_dt_bias, ssd_a_log, ssd_d, ssd_norm_g, ssd_out_w, attn_in_w, attn_q_norm_g, attn_k_norm_g, attn_lq1, attn_lk1, attn_lq2, attn_lk2, attn_subln_g, attn_out_w, ffn_up_w, ffn_conv_w, ffn_conv_b, ffn_down_w):
    batch, seq, d = x.shape
    depth = norm_mix_g.shape[0]
    assert d == D_MODEL and seq % ROW_TILE == 0 and ROW_TILE % ATTN_TILE == 0 and ATTN_TILE % CHUNK == 0
    assert ssd_conv_w.shape[1] - 1 <= V7X_SUBLANES and ffn_conv_w.shape[1] - 1 <= V7X_SUBLANES
    x2 = x.reshape(batch * seq, d)
    tables = _rope_tables(positions)

    def mixer_weights(i):
        j = i // N_MIXERS
        return [(ssd_out_w, j)] if i % N_MIXERS == 0 else [(attn_in_w, j), (attn_out_w, j)]

    ssd_in_bf = ssd_in_w.astype(BF16)
    mix_w, ffn_w = [None], None
    for i in range(depth):
        j = i // N_MIXERS
        if i % N_MIXERS == 0:
            first_casts = [(ssd_out_w, 0), (ffn_up_w, 0), (ffn_down_w, 0)] if i == 0 else []
            x2, cast = _ssd_layer(x2, batch, seq, norm_mix_g[i], ssd_in_bf, j, ssd_conv_w[j], ssd_conv_b[j],
                                  ssd_dt_bias[j], ssd_a_log[j], ssd_d[j], ssd_norm_g[j], mix_w[0], first_casts)
            if i == 0:
                ffn_w = cast
        else:
            lambda_init = 0.8 - 0.6 * math.exp(-0.3 * i)
            x2 = _attn_layer(x2, batch, seq, tables, norm_mix_g[i], mix_w[0], attn_q_norm_g[j],
                             attn_k_norm_g[j], attn_lq1[j], attn_lk1[j], attn_lq2[j], attn_lk2[j],
                             attn_subln_g[j], mix_w[1], lambda_init)
        casts = mixer_weights(i + 1) + [(ffn_up_w, i + 1), (ffn_down_w, i + 1)] if i + 1 < depth else []
        x2, cast = _ffn_layer(x2, seq, norm_ffn_g[i], ffn_w[0], ffn_conv_w[i], ffn_conv_b[i], ffn_w[1], casts)
        mix_w, ffn_w = cast[:-2], cast[-2:]
    return x2.reshape(batch, seq, d)
```
